```python
import math
import jax, jax.numpy as jnp
from jax import lax
import numpy as np

D_MODEL = 1024
BATCH = 8
SEQ = 2048
DEPTH = 1

GRID_W = 64
HEAD_DIM = 64
N_HEADS_DN = D_MODEL // (2 * HEAD_DIM)
N_HEADS_NA = D_MODEL // (2 * HEAD_DIM)
DN_WIDTH = N_HEADS_DN * HEAD_DIM
NA_WIDTH = N_HEADS_NA * HEAD_DIM
MIX_WIDTH = DN_WIDTH + NA_WIDTH
DN_CONV = 5
DN_CHUNK = 64
NA_WIN_H = 8
NA_WIN_W = 16
N_GROUPS = 4
EXPERTS_PER_GROUP = 8
N_EXPERTS = N_GROUPS * EXPERTS_PER_GROUP
TOP_K_EXPERT = 2
D_EXPERT = 512
EPS = 1e-6
IN_SPLITS = (DN_WIDTH, DN_WIDTH, DN_WIDTH, DN_WIDTH, N_HEADS_DN, N_HEADS_DN, N_HEADS_DN, N_HEADS_DN, NA_WIDTH, NA_WIDTH, NA_WIDTH)
IN_WIDTH = 4 * DN_WIDTH + 4 * N_HEADS_DN + 3 * NA_WIDTH

kernel_name = "hybrid_deltanet_natten_hmoe_block"


def rmsnorm(x, w):
    x32 = x.astype(jnp.float32)
    y = x32 * lax.rsqrt(jnp.mean(x32 * x32, axis=-1, keepdims=True) + EPS)
    return (y * w.astype(jnp.float32)).astype(x.dtype)


def l2norm(x):
    x32 = x.astype(jnp.float32)
    return x32 * lax.rsqrt(jnp.sum(x32 * x32, axis=-1, keepdims=True) + EPS)


def centred_depthwise_conv(x, w):
    k_w, ch = w.shape
    pad = (k_w - 1) // 2
    return lax.conv_general_dilated(x, w[:, None, :].astype(x.dtype), window_strides=(1,), padding=[(pad, pad)],
                                    dimension_numbers=("NWC", "WIO", "NWC"), feature_group_count=ch)


def gated_delta_rule_chunked(q, k, v, g, beta):
    B, H, S, dk = q.shape
    dv = v.shape[-1]
    n = S // DN_CHUNK

    def chunks(t):
        return t.reshape(B, H, n, DN_CHUNK, *t.shape[3:])

    q = chunks(q * dk ** -0.5)
    k, v, g, beta = chunks(k), chunks(v), chunks(g), chunks(beta)
    g_cum = jnp.cumsum(g, axis=-1)
    incl = jnp.tril(jnp.ones((DN_CHUNK, DN_CHUNK), dtype=bool))
    strict = jnp.tril(jnp.ones((DN_CHUNK, DN_CHUNK), dtype=bool), k=-1)
    diff = g_cum[..., :, None] - g_cum[..., None, :]
    decay = jnp.where(incl, jnp.exp(jnp.where(incl, diff, 0.0)), 0.0)
    k_beta = k * beta[..., None]
    m = jnp.where(strict, jnp.einsum("bhnid,bhnjd->bhnij", k_beta, k) * decay, 0.0)
    eye = jnp.eye(DN_CHUNK, dtype=q.dtype)
    t_inv = lax.linalg.triangular_solve(m + eye, jnp.broadcast_to(eye, m.shape), left_side=True,
                                        lower=True, unit_diagonal=True)
    u = jnp.einsum("bhnij,bhnje->bhnie", t_inv, v * beta[..., None])
    w = jnp.einsum("bhnij,bhnjd->bhnid", t_inv, k_beta * jnp.exp(g_cum)[..., None])
    attn = jnp.where(incl, jnp.einsum("bhnid,bhnjd->bhnij", q, k) * decay, 0.0)
    q_dec = q * jnp.exp(g_cum)[..., None]
    g_last = g_cum[..., -1]
    k_tail = k * jnp.exp(g_last[..., None] - g_cum)[..., None]

    def step(state, xs):
        q_c, k_c, u_c, w_c, a_c, gl_c = xs
        v_new = u_c - jnp.einsum("bhcd,bhde->bhce", w_c, state)
        o_c = jnp.einsum("bhcd,bhde->bhce", q_c, state) + jnp.einsum("bhij,bhje->bhie", a_c, v_new)
        state = state * jnp.exp(gl_c)[..., None, None] + jnp.einsum("bhcd,bhce->bhde", k_c, v_new)
        return state, o_c

    xs = (jnp.moveaxis(q_dec, 2, 0), jnp.moveaxis(k_tail, 2, 0), jnp.moveaxis(u, 2, 0),
          jnp.moveaxis(w, 2, 0), jnp.moveaxis(attn, 2, 0), jnp.moveaxis(g_last, 2, 0))
    state0 = jnp.zeros((B, H, dk, dv), q.dtype)
    _, o = lax.scan(step, state0, xs)
    return jnp.moveaxis(o, 0, 2).reshape(B, H, S, dv)


def neighbourhood_attention(q, k, v, rpb):
    B, S, H, d = q.shape
    rows = S // GRID_W
    win_h = min(NA_WIN_H, rows)
    r = np.arange(rows)
    r0 = np.clip(r - win_h // 2, 0, rows - win_h)
    key_rows = r0[:, None] + np.arange(win_h)[None, :]
    cc = np.arange(GRID_W)
    c0 = np.clip(cc - NA_WIN_W // 2, 0, GRID_W - NA_WIN_W)
    key_cols = c0[:, None] + np.arange(NA_WIN_W)[None, :]
    row_off = key_rows - r[:, None] + NA_WIN_H - 1
    col_off = key_cols - cc[:, None] + NA_WIN_W - 1
    bias = rpb[:, row_off[:, :, None, None], col_off[None, None, :, :]]
    bias = bias.transpose(1, 0, 3, 2, 4)
    qg = (q * d ** -0.5).reshape(B, rows, GRID_W, H, d).transpose(1, 0, 2, 3, 4)
    kg = k.reshape(B, rows, GRID_W, H, d)
    vg = v.reshape(B, rows, GRID_W, H, d)

    def row_attend(xs):
        q_r, rows_i, bias_r = xs
        k_nb = kg[:, rows_i][:, :, key_cols]
        v_nb = vg[:, rows_i][:, :, key_cols]
        s = jnp.einsum("bqhd,brqwhd->bhqrw", q_r, k_nb).astype(jnp.float32) + bias_r[None].astype(jnp.float32)
        p = jax.nn.softmax(s.reshape(B, H, GRID_W, -1), axis=-1).reshape(s.shape)
        return jnp.einsum("bhqrw,brqwhd->bqhd", p.astype(v.dtype), v_nb)

    o = lax.map(row_attend, (qg, jnp.asarray(key_rows), bias))
    return o.transpose(1, 0, 2, 3, 4).reshape(B, S, H * d)


def token_mixer(h, w_in, w_conv_dn, a_log_dn, dt_bias_dn, g_onorm_dn, rpb_na, w_out):
    B, S, _ = h.shape
    proj = h @ w_in
    qkv_dn = jax.nn.silu(centred_depthwise_conv(proj[..., :3 * DN_WIDTH], w_conv_dn))
    q_dn, k_dn, v_dn = jnp.split(qkv_dn, 3, axis=-1)
    _, _, _, z_dn, a_f, a_b, b_f, b_b, q_na, k_na, v_na = jnp.split(proj, np.cumsum(IN_SPLITS)[:-1].tolist(), axis=-1)

    def heads(t):
        return t.reshape(B, S, -1, HEAD_DIM).transpose(0, 2, 1, 3)

    qa, ka = l2norm(heads(q_dn)), l2norm(heads(k_dn))
    va = heads(v_dn).astype(jnp.float32)

    def log_decay(a, i):
        a32 = a.astype(jnp.float32).transpose(0, 2, 1)
        return -jnp.exp(a_log_dn[i].astype(jnp.float32))[None, :, None] * jax.nn.softplus(
            a32 + dt_bias_dn[i].astype(jnp.float32)[None, :, None])

    g_f, g_b = log_decay(a_f, 0), log_decay(a_b, 1)
    beta_f = jax.nn.sigmoid(b_f.astype(jnp.float32).transpose(0, 2, 1))
    beta_b = jax.nn.sigmoid(b_b.astype(jnp.float32).transpose(0, 2, 1))
    o_f = gated_delta_rule_chunked(qa, ka, va, g_f, beta_f)

    def flip(t):
        return jnp.flip(t, axis=2)

    o_b = flip(gated_delta_rule_chunked(flip(qa), flip(ka), flip(va), flip(g_b), flip(beta_b)))
    o_dn = (o_f + o_b).transpose(0, 2, 1, 3)
    o_dn = rmsnorm(o_dn, g_onorm_dn) * jax.nn.silu(z_dn.reshape(B, S, N_HEADS_DN, HEAD_DIM).astype(jnp.float32))
    o_dn = o_dn.reshape(B, S, DN_WIDTH).astype(h.dtype)

    o_na = neighbourhood_attention(q_na.reshape(B, S, N_HEADS_NA, HEAD_DIM), k_na.reshape(B, S, N_HEADS_NA, HEAD_DIM),
                                   v_na.reshape(B, S, N_HEADS_NA, HEAD_DIM), rpb_na)
    return jnp.concatenate([o_dn, o_na.astype(h.dtype)], axis=-1) @ w_out


def hierarchical_moe(h, w_group, b_group, w_expert, b_expert, w_gate, w_up, w_down):
    B, S, D = h.shape
    t = h.reshape(B * S, D)
    group_p = jax.nn.softmax((t @ w_group + b_group).astype(jnp.float32), axis=-1)
    gp_top, g_idx = lax.top_k(group_p, 1)
    exp_logits = (t @ w_expert + b_expert).astype(jnp.float32).reshape(-1, N_GROUPS, EXPERTS_PER_GROUP)
    in_group = jnp.einsum("tg,tge->te", jax.nn.one_hot(g_idx[:, 0], N_GROUPS, dtype=jnp.float32), exp_logits)
    ep_top, e_idx = lax.top_k(jax.nn.softmax(in_group, axis=-1), TOP_K_EXPERT)
    weights = gp_top * ep_top / jnp.sum(ep_top, axis=-1, keepdims=True)
    expert_id = g_idx * EXPERTS_PER_GROUP + e_idx
    combine = jnp.einsum("tk,tke->te", weights, jax.nn.one_hot(expert_id, N_EXPERTS, dtype=jnp.float32))
    y = jnp.zeros(t.shape, jnp.float32)
    for e in range(N_EXPERTS):
        he = jax.nn.silu(t @ w_gate[e]) * (t @ w_up[e])
        y = y + combine[:, e:e + 1] * (he @ w_down[e]).astype(jnp.float32)
    return y.astype(h.dtype).reshape(B, S, D)


def setup_inputs(seed: int = 0) -> dict:
    key = jax.random.key(seed)
    ks = jax.random.split(key, 24)
    f32 = jnp.float32

    def nrm(k, shape, scale):
        return jax.random.normal(k, shape, f32) * scale

    def gain(k, shape):
        return 1.0 + 0.05 * jax.random.normal(k, shape, f32)

    dt = jnp.exp(jax.random.uniform(ks[9], (DEPTH, 2, N_HEADS_DN), f32, math.log(1e-3), math.log(1e-1)))
    return {
        "x": nrm(ks[0], (BATCH, SEQ, D_MODEL), 1.0),
        "c": nrm(ks[1], (BATCH, D_MODEL), 1.0),
        "w_ada": nrm(ks[2], (DEPTH, D_MODEL, 6 * D_MODEL), 0.5 * D_MODEL ** -0.5),
        "b_ada": nrm(ks[3], (DEPTH, 6 * D_MODEL), 0.02),
        "g_pre_mix": gain(ks[4], (DEPTH, D_MODEL)),
        "g_post_mix": gain(ks[5], (DEPTH, D_MODEL)),
        "w_in": nrm(ks[6], (DEPTH, D_MODEL, IN_WIDTH), D_MODEL ** -0.5),
        "w_conv_dn": nrm(ks[7], (DEPTH, DN_CONV, 3 * DN_WIDTH), DN_CONV ** -0.5),
        "a_log_dn": jnp.log(jax.random.uniform(ks[8], (DEPTH, 2, N_HEADS_DN), f32, 1.0, 16.0)),
        "dt_bias_dn": dt + jnp.log(-jnp.expm1(-dt)),
        "g_onorm_dn": gain(ks[10], (DEPTH, HEAD_DIM)),
        "rpb_na": nrm(ks[11], (DEPTH, N_HEADS_NA, 2 * NA_WIN_H - 1, 2 * NA_WIN_W - 1), 0.1),
        "w_out": nrm(ks[12], (DEPTH, MIX_WIDTH, D_MODEL), MIX_WIDTH ** -0.5),
        "g_pre_ffn": gain(ks[13], (DEPTH, D_MODEL)),
        "g_post_ffn": gain(ks[14], (DEPTH, D_MODEL)),
        "w_group": nrm(ks[15], (DEPTH, D_MODEL, N_GROUPS), D_MODEL ** -0.5),
        "b_group": nrm(ks[16], (DEPTH, N_GROUPS), 0.01),
        "w_expert": nrm(ks[17], (DEPTH, D_MODEL, N_EXPERTS), D_MODEL ** -0.5),
        "b_expert": nrm(ks[18], (DEPTH, N_EXPERTS), 0.01),
        "w_gate": nrm(ks[19], (DEPTH, N_EXPERTS, D_MODEL, D_EXPERT), D_MODEL ** -0.5),
        "w_up": nrm(ks[20], (DEPTH, N_EXPERTS, D_MODEL, D_EXPERT), D_MODEL ** -0.5),
        "w_down": nrm(ks[21], (DEPTH, N_EXPERTS, D_EXPERT, D_MODEL), D_EXPERT ** -0.5),
    }


def reference(x, c, w_ada, b_ada, g_pre_mix, g_post_mix, w_in, w_conv_dn, a_log_dn, dt_bias_dn, g_onorm_dn,
              rpb_na, w_out, g_pre_ffn, g_post_ffn, w_group, b_group, w_expert, b_expert, w_gate, w_up, w_down):
    cond = jax.nn.silu(c)
    for l in range(DEPTH):
        ada = cond @ w_ada[l] + b_ada[l]
        sh1, sc1, gt1, sh2, sc2, gt2 = jnp.split(ada[:, None, :], 6, axis=-1)
        h = rmsnorm(x, g_pre_mix[l]) * (1.0 + sc1) + sh1
        y = token_mixer(h, w_in[l], w_conv_dn[l], a_log_dn[l], dt_bias_dn[l], g_onorm_dn[l], rpb_na[l], w_out[l])
        x = x + gt1 * rmsnorm(y, g_post_mix[l])
        h = rmsnorm(x, g_pre_ffn[l]) * (1.0 + sc2) + sh2
        y = hierarchical_moe(h, w_group[l], b_group[l], w_expert[l], b_expert[l], w_gate[l], w_up[l], w_down[l])
        x = x + gt2 * rmsnorm(y, g_post_ffn[l])
    return x
```

```python
import numpy as np
import jax
import jax.numpy as jnp
from jax import lax
from jax.experimental import pallas as pl
from jax.experimental.pallas import tpu as pltpu

F32 = jnp.float32
BF16 = jnp.bfloat16
HIGHEST = lax.Precision.HIGHEST

GRID_W = 64
HEAD_DIM = 64
DN_CONV = 5
DN_CHUNK = 64
NA_WIN_H = 8
NA_WIN_W = 16
N_GROUPS = 4
EXPERTS_PER_GROUP = 8
N_EXPERTS = N_GROUPS * EXPERTS_PER_GROUP
EPS = 1e-6
LANES = 128
NEG_BIG = -1e30
VMEM_LIMIT = 56 * 1024 * 1024


def _cparams(*sem):
    return pltpu.CompilerParams(dimension_semantics=sem, vmem_limit_bytes=VMEM_LIMIT)


def _sigmoid(x):
    return 1.0 / (1.0 + jnp.exp(-x))


def _silu(x):
    return x * _sigmoid(x)


def _softplus(x):
    return jnp.maximum(x, 0.0) + jnp.log1p(jnp.exp(-jnp.abs(x)))


def _dot(a, b, **kw):
    return jnp.dot(a, b, preferred_element_type=F32, **kw)


def _dot_nt(a, b, **kw):
    return lax.dot_general(a, b, (((1,), (1,)), ((), ())), preferred_element_type=F32, **kw)


def _dot_tn(a, b, **kw):
    return lax.dot_general(a, b, (((0,), (0,)), ((), ())), preferred_element_type=F32, **kw)


def _rms(x):
    return x * lax.rsqrt(jnp.mean(x * x, axis=-1, keepdims=True) + EPS)


def _iota2(shape):
    return lax.broadcasted_iota(jnp.int32, shape, 0), lax.broadcasted_iota(jnp.int32, shape, 1)


def _ada_kernel(c_ref, w_ref, b_ref, o_ref):
    cond = _silu(c_ref[...])
    o_ref[...] = _dot(cond, w_ref[...], precision=HIGHEST) + b_ref[...]


def _ada(c, w_ada, b_ada):
    bsz, d = c.shape
    n = w_ada.shape[1]
    tn = 1536
    return pl.pallas_call(
        _ada_kernel,
        grid=(n // tn,),
        in_specs=[pl.BlockSpec((bsz, d), lambda j: (0, 0)),
                  pl.BlockSpec((d, tn), lambda j: (0, j)),
                  pl.BlockSpec((1, tn), lambda j: (0, j))],
        out_specs=pl.BlockSpec((bsz, tn), lambda j: (0, j)),
        out_shape=jax.ShapeDtypeStruct((bsz, n), F32),
        compiler_params=_cparams("arbitrary"),
    )(c, w_ada, b_ada.reshape(1, n))


def _inproj_kernel(x_ref, sc_ref, sh_ref, g_ref, wdn_ref, wz_ref, wab_ref, wna_ref, alog_ref, dtb_ref,
                   dn_ref, z_ref, gate_ref, na_ref):
    h = _rms(x_ref[0]) * g_ref[...]
    h = h * (1.0 + sc_ref[0]) + sh_ref[0]
    hb = h.astype(BF16)
    dn_ref[0] = _dot(hb, wdn_ref[...]).astype(BF16)
    z_ref[0] = _dot(hb, wz_ref[...]).astype(BF16)
    na_ref[0] = _dot(hb, wna_ref[...]).astype(BF16)
    ab = _dot(hb, wab_ref[...])
    lane = lax.broadcasted_iota(jnp.int32, ab.shape, 1)
    decay = -jnp.exp(alog_ref[...]) * _softplus(ab + dtb_ref[...])
    gate_ref[0] = jnp.where(lane < 16, decay, _sigmoid(ab))


def _inproj(x, sc, sh, g, wdn, wz, wab, wna, alog, dtb, tm=512):
    bsz, s, d = x.shape
    ndn, nz, nna = wdn.shape[1], wz.shape[1], wna.shape[1]
    full = lambda a: pl.BlockSpec(a.shape, lambda b, i: (0,) * a.ndim)
    mod = pl.BlockSpec((1, 1, d), lambda b, i: (b, 0, 0))
    tok = lambda n: pl.BlockSpec((1, tm, n), lambda b, i: (b, i, 0))
    return pl.pallas_call(
        _inproj_kernel,
        grid=(bsz, s // tm),
        in_specs=[tok(d), mod, mod, full(g), full(wdn), full(wz), full(wab), full(wna), full(alog), full(dtb)],
        out_specs=[tok(ndn), tok(nz), tok(LANES), tok(nna)],
        out_shape=[jax.ShapeDtypeStruct((bsz, s, ndn), BF16), jax.ShapeDtypeStruct((bsz, s, nz), BF16),
                   jax.ShapeDtypeStruct((bsz, s, LANES), F32), jax.ShapeDtypeStruct((bsz, s, nna), BF16)],
        compiler_params=_cparams("arbitrary", "arbitrary"),
    )(x, sc, sh, g, wdn, wz, wab, wna, alog, dtb)


def _conv_silu(x, w):
    s = x.shape[0]
    row = lax.broadcasted_iota(jnp.int32, x.shape, 0)
    pad = (DN_CONV - 1) // 2
    acc = x * w[pad:pad + 1]
    for j in range(DN_CONV):
        d = j - pad
        if d == 0:
            continue
        xs = pltpu.roll(x, (-d) % s, 0)
        ok = (row + d >= 0) & (row + d < s)
        acc = acc + jnp.where(ok, xs, 0.0) * w[j:j + 1]
    return _silu(acc)


def _inv_unit_triangular(m, eye, ri, rj):
    n = m.shape[0]
    same = lambda sh: lax.shift_right_logical(ri, sh) == lax.shift_right_logical(rj, sh)
    t = eye - jnp.where(same(1), m, 0.0)
    sh = 1
    while (1 << sh) < n:
        cm = jnp.where(same(sh + 1) & jnp.logical_not(same(sh)), m, 0.0).astype(BF16)
        tb = t.astype(BF16)
        t = t - _dot(tb, _dot(cm, tb).astype(BF16))
        sh += 1
    return t


def _delta_kernel(q_ref, k_ref, v_ref, wq_ref, wk_ref, wv_ref, gate_ref, gt_ref, z_ref, gon_ref, o_ref,
                  gcr_s, k_s, q_s, kb_s, kbg_s, qd_s, kt_s, vb_s, gci_s, egl_s, t_s, at_s, o0_s, o1_s):
    hp = pl.program_id(1)
    s = q_ref.shape[1]
    c = DN_CHUNK
    hd = HEAD_DIM
    n_chunks = s // c
    blk = 256

    li, lj = _iota2((LANES, LANES))
    same_head = (lax.shift_right_logical(li, 6) == lax.shift_right_logical(lj, 6)).astype(F32)

    def l2n(x):
        return x * lax.rsqrt(_dot(x * x, same_head, precision=HIGHEST) + EPS)

    q2 = l2n(_conv_silu(q_ref[0].astype(F32), wq_ref[...])) * (hd ** -0.5)
    k2 = l2n(_conv_silu(k_ref[0].astype(F32), wk_ref[...]))
    v2 = _conv_silu(v_ref[0].astype(F32), wv_ref[...])
    gates = gate_ref[0]

    def bcast_cols(base):
        sel = (li == base + 2 * hp + lax.shift_right_logical(lj, 6)).astype(F32)
        return _dot(gates, sel, precision=HIGHEST)

    ri, rj = _iota2((c, c))
    eye = (ri == rj).astype(F32)
    bi, bj = _iota2((blk, blk))
    same_chunk = lax.shift_right_logical(bi, 6) == lax.shift_right_logical(bj, 6)
    chunk_ones = same_chunk.astype(F32)

    o0_s[...] = jnp.zeros_like(o0_s)
    o1_s[...] = jnp.zeros_like(o1_s)

    for direction in range(2):
        if direction == 0:
            incl, strict = ri >= rj, ri > rj
            cum_blk = (same_chunk & (bi >= bj)).astype(F32)
            cum_row = (ri <= rj).astype(F32)
        else:
            incl, strict = ri <= rj, ri < rj
            cum_blk = (same_chunk & (bi <= bj)).astype(F32)
            cum_row = (ri >= rj).astype(F32)

        g2 = bcast_cols(8 * direction)
        beta2 = bcast_cols(16 + 8 * direction)
        gci2 = jnp.concatenate(
            [_dot(cum_blk, g2[r * blk:(r + 1) * blk], precision=HIGHEST) for r in range(s // blk)], axis=0)
        gl2 = jnp.concatenate(
            [_dot(chunk_ones, g2[r * blk:(r + 1) * blk], precision=HIGHEST) for r in range(s // blk)], axis=0)
        egc2 = jnp.exp(gci2)
        kb2 = k2 * beta2
        kbg2 = kb2 * egc2
        qd2 = q2 * egc2
        kt2 = k2 * jnp.exp(gl2 - gci2)
        vb2 = v2 * beta2
        egl2 = jnp.exp(gl2)

        gcr_s[...] = _dot(gt_ref[0].reshape(n_chunks * 32, c), cum_row, precision=HIGHEST).reshape(n_chunks, 32, c)

        for hh in range(2):
            sl = slice(hh * hd, (hh + 1) * hd)
            o_s = o0_s if hh == 0 else o1_s
            gate_row = 8 * direction + 2 * hp + hh
            k_s[...] = k2[:, sl].astype(BF16)
            q_s[...] = q2[:, sl].astype(BF16)
            kb_s[...] = kb2[:, sl].astype(BF16)
            kbg_s[...] = kbg2[:, sl].astype(BF16)
            qd_s[...] = qd2[:, sl].astype(BF16)
            kt_s[...] = kt2[:, sl].astype(BF16)
            vb_s[...] = vb2[:, sl]
            gci_s[...] = gci2[:, sl]
            egl_s[...] = egl2[:, sl]

            def chunk_prep(n, carry):
                rows = pl.ds(pl.multiple_of(n * c, c), c)
                kc = k_s[rows, :]
                gcj = jnp.broadcast_to(gcr_s[n, pl.ds(gate_row, 1), :], (c, c))
                diff = gci_s[rows, :] - gcj
                decay = jnp.where(incl, jnp.exp(jnp.where(incl, diff, 0.0)), 0.0)
                m = jnp.where(strict, _dot_nt(kb_s[rows, :], kc) * decay, 0.0)
                t_s[rows, :] = _inv_unit_triangular(m, eye, ri, rj).astype(BF16)
                at_s[rows, :] = (_dot_nt(q_s[rows, :], kc) * decay).astype(BF16)
                return carry

            lax.fori_loop(0, n_chunks, chunk_prep, 0, unroll=2)

            def scan_step(i, state):
                n = i if direction == 0 else n_chunks - 1 - i
                r0 = pl.multiple_of(n * c, c)
                rows = pl.ds(r0, c)
                sb = state.astype(BF16)
                rhs = vb_s[rows, :] - _dot(kbg_s[rows, :], sb)
                v_new = _dot(t_s[rows, :], rhs.astype(BF16)).astype(BF16)
                o_s[rows, :] += _dot(qd_s[rows, :], sb) + _dot(at_s[rows, :], v_new)
                return state * egl_s[pl.ds(r0, 1), :] + _dot_tn(kt_s[rows, :], v_new)

            lax.fori_loop(0, n_chunks, scan_step, jnp.zeros((hd, hd), F32))

    o2 = jnp.concatenate([o0_s[...], o1_s[...]], axis=1)
    ms = _dot(o2 * o2, same_head, precision=HIGHEST) * (1.0 / hd)
    o2 = o2 * lax.rsqrt(ms + EPS) * gon_ref[...]
    o_ref[0] = (o2 * _silu(z_ref[0].astype(F32))).astype(BF16)


def _delta(dn, w_conv, gates, gt, z, gon2):
    bsz, s, _ = dn.shape
    n_pairs = z.shape[2] // LANES
    c = DN_CHUNK
    col = lambda off: pl.BlockSpec((1, s, LANES), lambda b, p: (b, 0, off + p))
    wcol = lambda off: pl.BlockSpec((DN_CONV, LANES), lambda b, p: (0, off + p))
    half = lambda dt: pltpu.VMEM((s, HEAD_DIM), dt)
    return pl.pallas_call(
        _delta_kernel,
        grid=(bsz, n_pairs),
        in_specs=[col(0), col(n_pairs), col(2 * n_pairs), wcol(0), wcol(n_pairs), wcol(2 * n_pairs),
                  pl.BlockSpec((1, s, LANES), lambda b, p: (b, 0, 0)),
                  pl.BlockSpec((1, s // c, 32, c), lambda b, p: (b, 0, 0, 0)),
                  col(0),
                  pl.BlockSpec((1, LANES), lambda b, p: (0, 0))],
        out_specs=col(0),
        out_shape=jax.ShapeDtypeStruct((bsz, s, n_pairs * LANES), BF16),
        scratch_shapes=[pltpu.VMEM((s // c, 32, c), F32),
                        half(BF16), half(BF16), half(BF16), half(BF16), half(BF16), half(BF16),
                        half(F32), half(F32), half(F32), half(BF16), half(BF16), half(F32), half(F32)],
        compiler_params=_cparams("arbitrary", "arbitrary"),
    )(dn, dn, dn, w_conv, w_conv, w_conv, gates, gt, z, gon2)


def _na_bias_tables(rpb):
    n_heads = rpb.shape[0]
    cc = np.arange(GRID_W)
    c0 = np.clip(cc - NA_WIN_W // 2, 0, GRID_W - NA_WIN_W)
    kc = np.arange(GRID_W)
    in_win = (kc[None, :] >= c0[:, None]) & (kc[None, :] < c0[:, None] + NA_WIN_W)
    col_off = np.clip(kc[None, :] - cc[:, None] + NA_WIN_W - 1, 0, 2 * NA_WIN_W - 2)
    d = np.arange(NA_WIN_H)
    row_off = np.arange(NA_WIN_H)[None, :] - d[:, None] + NA_WIN_H - 1
    bias = rpb[:, row_off[:, None, :, None], col_off[None, :, None, :]]
    bias = jnp.where(in_win[None, None, :, None, :], bias.astype(F32), NEG_BIG)
    return bias.reshape(n_heads, NA_WIN_H, GRID_W, NA_WIN_H * GRID_W)


def _natten_kernel(q_ref, k_ref, v_ref, bias_ref, o_ref):
    s = q_ref.shape[1]
    rows = s // GRID_W
    w = GRID_W
    win = min(NA_WIN_H, rows)
    hd = HEAD_DIM
    lane = lax.broadcasted_iota(jnp.int32, (w, LANES), 1)
    first = lane < hd
    for r in range(rows):
        r0 = min(max(r - win // 2, 0), rows - win)
        d = r - r0
        q = q_ref[0, r * w:(r + 1) * w, :]
        zero = jnp.zeros_like(q)
        q_st = jnp.concatenate([jnp.where(first, q, zero), jnp.where(first, zero, q)], axis=0)
        kw = k_ref[0, r0 * w:(r0 + win) * w, :]
        vw = v_ref[0, r0 * w:(r0 + win) * w, :]
        sc = _dot_nt(q_st, kw) + bias_ref[0, d]
        sc = sc - jnp.max(sc, axis=-1, keepdims=True)
        p = jnp.exp(sc)
        denom = jnp.sum(p, axis=-1, keepdims=True)
        o = _dot(p.astype(BF16), vw) / denom
        o_ref[0, r * w:(r + 1) * w, :] = jnp.where(first, o[:w], o[w:]).astype(BF16)


def _natten(na, bias):
    bsz, s, n3 = na.shape
    n_pairs = n3 // (3 * LANES)
    col = lambda off: pl.BlockSpec((1, s, LANES), lambda p, b: (b, 0, off + p))
    return pl.pallas_call(
        _natten_kernel,
        grid=(n_pairs, bsz),
        in_specs=[col(0), col(n_pairs), col(2 * n_pairs),
                  pl.BlockSpec((1,) + bias.shape[1:], lambda p, b: (p, 0, 0, 0))],
        out_specs=col(0),
        out_shape=jax.ShapeDtypeStruct((bsz, s, n_pairs * LANES), BF16),
        compiler_params=_cparams("arbitrary", "arbitrary"),
    )(na, na, na, bias)


def _outproj_kernel(odn_ref, ona_ref, x_ref, wo_dn_ref, wo_na_ref, gpost_ref, gt1_ref, gpre_ref, sc_ref, sh_ref,
                    wr_ref, br_ref, x1_ref, h_ref, comb_ref):
    y = _dot(odn_ref[0], wo_dn_ref[...]) + _dot(ona_ref[0], wo_na_ref[...])
    x1 = x_ref[0] + gt1_ref[0] * (_rms(y) * gpost_ref[...])
    x1_ref[0] = x1
    h = _rms(x1) * gpre_ref[...]
    h = h * (1.0 + sc_ref[0]) + sh_ref[0]
    h_ref[0] = h.astype(BF16)

    logits = _dot(h, wr_ref[...], precision=HIGHEST) + br_ref[...]
    lane = lax.broadcasted_iota(jnp.int32, logits.shape, 1).astype(F32)
    big = float(LANES)

    def masked_softmax(mask):
        lm = jnp.where(mask, logits, NEG_BIG)
        e = jnp.where(mask, jnp.exp(lm - jnp.max(lm, axis=-1, keepdims=True)), 0.0)
        return e / jnp.sum(e, axis=-1, keepdims=True)

    def top1(p, mask):
        pm = jnp.where(mask, p, -1.0)
        best = jnp.max(pm, axis=-1, keepdims=True)
        idx = jnp.min(jnp.where(mask & (pm == best), lane, big), axis=-1, keepdims=True)
        return best, idx

    gmask = lane < N_GROUPS
    gp_top, g_idx = top1(masked_softmax(gmask), gmask)
    e_lo = N_GROUPS + g_idx * EXPERTS_PER_GROUP
    emask = (lane >= e_lo) & (lane < e_lo + EXPERTS_PER_GROUP)
    pe = masked_softmax(emask)
    p1, i1 = top1(pe, emask)
    p2, i2 = top1(pe, emask & (lane != i1))
    w1 = gp_top * p1 / (p1 + p2)
    w2 = gp_top * p2 / (p1 + p2)
    el = lane + N_GROUPS
    comb_ref[0] = jnp.where(el == i1, w1, 0.0) + jnp.where(el == i2, w2, 0.0)


def _outproj(odn, ona, x, wo_dn, wo_na, gpost, gt1, gpre, sc2, sh2, wr, br, tm=512):
    bsz, s, d = x.shape
    full = lambda a: pl.BlockSpec(a.shape, lambda b, i: (0,) * a.ndim)
    mod = pl.BlockSpec((1, 1, d), lambda b, i: (b, 0, 0))
    tok = lambda n: pl.BlockSpec((1, tm, n), lambda b, i: (b, i, 0))
    return pl.pallas_call(
        _outproj_kernel,
        grid=(bsz, s // tm),
        in_specs=[tok(odn.shape[2]), tok(ona.shape[2]), tok(d), full(wo_dn), full(wo_na), full(gpost), mod,
                  full(gpre), mod, mod, full(wr), full(br)],
        out_specs=[tok(d), tok(d), tok(LANES)],
        out_shape=[jax.ShapeDtypeStruct((bsz, s, d), F32), jax.ShapeDtypeStruct((bsz, s, d), BF16),
                   jax.ShapeDtypeStruct((bsz, s, LANES), F32)],
        compiler_params=_cparams("arbitrary", "arbitrary"),
    )(odn, ona, x, wo_dn, wo_na, gpost, gt1, gpre, sc2, sh2, wr, br)


def _moe_kernel(h_ref, comb_ref, wg_ref, wu_ref, wd_ref, x1_ref, gpost_ref, gt2_ref, o_ref, acc_ref):
    e = pl.program_id(2)

    @pl.when(e == 0)
    def _():
        acc_ref[...] = jnp.zeros_like(acc_ref)

    h = h_ref[0]
    gate = _dot(h, wg_ref[0].astype(BF16))
    up = _dot(h, wu_ref[0].astype(BF16))
    he = (_silu(gate) * up).astype(BF16)
    out = _dot(he, wd_ref[0].astype(BF16))
    comb = comb_ref[0]
    lane = lax.broadcasted_iota(jnp.int32, comb.shape, 1)
    w = jnp.sum(jnp.where(lane == e, comb, 0.0), axis=-1, keepdims=True)
    acc_ref[...] += w * out

    @pl.when(e == pl.num_programs(2) - 1)
    def _():
        o_ref[0] = x1_ref[0] + gt2_ref[0] * (_rms(acc_ref[...]) * gpost_ref[...])


def _moe(h, comb, w_gate, w_up, w_down, x1, gpost, gt2, tm=1024):
    bsz, s, d = x1.shape
    n_exp, _, de = w_gate.shape
    tok = lambda n: pl.BlockSpec((1, tm, n), lambda b, i, e: (b, i, 0))
    return pl.pallas_call(
        _moe_kernel,
        grid=(bsz, s // tm, n_exp),
        in_specs=[tok(d), tok(LANES),
                  pl.BlockSpec((1, d, de), lambda b, i, e: (e, 0, 0)),
                  pl.BlockSpec((1, d, de), lambda b, i, e: (e, 0, 0)),
                  pl.BlockSpec((1, de, d), lambda b, i, e: (e, 0, 0)),
                  tok(d),
                  pl.BlockSpec((1, d), lambda b, i, e: (0, 0)),
                  pl.BlockSpec((1, 1, d), lambda b, i, e: (b, 0, 0))],
        out_specs=tok(d),
        out_shape=jax.ShapeDtypeStruct((bsz, s, d), F32),
        scratch_shapes=[pltpu.VMEM((tm, d), F32)],
        compiler_params=_cparams("arbitrary", "arbitrary", "arbitrary"),
    )(h, comb, w_gate, w_up, w_down, x1, gpost, gt2)


def _pad_lanes(a, n=LANES):
    return jnp.pad(a, [(0, 0)] * (a.ndim - 1) + [(0, n - a.shape[-1])])


def kernel(x, c, w_ada, b_ada, g_pre_mix, g_post_mix, w_in, w_conv_dn, a_log_dn, dt_bias_dn, g_onorm_dn, rpb_na,
           w_out, g_pre_ffn, g_post_ffn, w_group, b_group, w_expert, b_expert, w_gate, w_up, w_down):
    bsz, s, d = x.shape
    depth = w_ada.shape[0]
    n_dn = a_log_dn.shape[2]
    dn_w = n_dn * HEAD_DIM
    for l in range(depth):
        ada = _ada(c, w_ada[l], b_ada[l])
        sh1, sc1, gt1, sh2, sc2, gt2 = [a.reshape(bsz, 1, d) for a in jnp.split(ada, 6, axis=-1)]

        wi = w_in[l]
        o_z, o_ab, o_na = 3 * dn_w, 4 * dn_w, 4 * dn_w + 4 * n_dn
        wdn = wi[:, :o_z].astype(BF16)
        wz = wi[:, o_z:o_ab].astype(BF16)
        wab = _pad_lanes(wi[:, o_ab:o_na]).astype(BF16)
        na_w = (wi.shape[1] - o_na) // 3
        wna = jnp.concatenate([wi[:, o_na:o_na + na_w] * (HEAD_DIM ** -0.5), wi[:, o_na + na_w:]], axis=1).astype(BF16)
        alog = _pad_lanes(a_log_dn[l].reshape(1, 2 * n_dn))
        dtb = _pad_lanes(dt_bias_dn[l].reshape(1, 2 * n_dn))
        dn, z, gates, na = _inproj(x, sc1, sh1, g_pre_mix[l].reshape(1, d), wdn, wz, wab, wna, alog, dtb)

        gt = gates[:, :, :32].reshape(bsz, s // DN_CHUNK, DN_CHUNK, 32).transpose(0, 1, 3, 2)
        gon2 = jnp.tile(g_onorm_dn[l].reshape(1, HEAD_DIM), (1, LANES // HEAD_DIM))
        o_dn = _delta(dn, w_conv_dn[l], gates, gt, z, gon2)

        bias = _na_bias_tables(rpb_na[l])
        n_pairs = bias.shape[0] // 2
        bias = bias.reshape(n_pairs, 2, NA_WIN_H, GRID_W, -1).transpose(0, 2, 1, 3, 4).reshape(
            n_pairs, NA_WIN_H, 2 * GRID_W, -1)
        o_na = _natten(na, bias)

        wo = w_out[l].astype(BF16)
        wr = _pad_lanes(jnp.concatenate([w_group[l], w_expert[l]], axis=1))
        br = _pad_lanes(jnp.concatenate([b_group[l], b_expert[l]]).reshape(1, -1))
        x1, h2, comb = _outproj(o_dn, o_na, x, wo[:dn_w], wo[dn_w:], g_post_mix[l].reshape(1, d), gt1,
                                g_pre_ffn[l].reshape(1, d), sc2, sh2, wr, br)
        x = _moe(h2, comb, w_gate[l], w_up[l], w_down[l], x1, g_post_ffn[l].reshape(1, d), gt2)
    return x
```

```python
import numpy as np
import jax
import jax.numpy as jnp
from jax import lax
from jax.experimental import pallas as pl
from jax.experimental.pallas import tpu as pltpu

F32 = jnp.float32
BF16 = jnp.bfloat16
HIGHEST = lax.Precision.HIGHEST

GRID_W = 64
HEAD_DIM = 64
DN_CONV = 5
DN_CHUNK = 128
NA_WIN_H = 8
NA_WIN_W = 16
N_GROUPS = 4
EXPERTS_PER_GROUP = 8
N_EXPERTS = N_GROUPS * EXPERTS_PER_GROUP
EPS = 1e-6
LANES = 128
NEG_BIG = -1e30
VMEM_LIMIT = 56 * 1024 * 1024


def _cparams(*sem):
    return pltpu.CompilerParams(dimension_semantics=sem, vmem_limit_bytes=VMEM_LIMIT)


def _sigmoid(x):
    return 1.0 / (1.0 + jnp.exp(-x))


def _silu(x):
    return x * _sigmoid(x)


def _softplus(x):
    return jnp.maximum(x, 0.0) + jnp.log1p(jnp.exp(-jnp.abs(x)))


def _dot(a, b, **kw):
    return jnp.dot(a, b, preferred_element_type=F32, **kw)


def _dot_nt(a, b, **kw):
    return lax.dot_general(a, b, (((1,), (1,)), ((), ())), preferred_element_type=F32, **kw)


def _dot_tn(a, b, **kw):
    return lax.dot_general(a, b, (((0,), (0,)), ((), ())), preferred_element_type=F32, **kw)


def _rms(x):
    return x * lax.rsqrt(jnp.mean(x * x, axis=-1, keepdims=True) + EPS)


def _iota2(shape):
    return lax.broadcasted_iota(jnp.int32, shape, 0), lax.broadcasted_iota(jnp.int32, shape, 1)


def _ada_kernel(c_ref, w_ref, b_ref, o_ref):
    cond = _silu(c_ref[...])
    o_ref[...] = _dot(cond, w_ref[...], precision=HIGHEST) + b_ref[...]


def _ada(c, w_ada, b_ada):
    bsz, d = c.shape
    n = w_ada.shape[1]
    tn = 1536
    return pl.pallas_call(
        _ada_kernel,
        grid=(n // tn,),
        in_specs=[pl.BlockSpec((bsz, d), lambda j: (0, 0)),
                  pl.BlockSpec((d, tn), lambda j: (0, j)),
                  pl.BlockSpec((1, tn), lambda j: (0, j))],
        out_specs=pl.BlockSpec((bsz, tn), lambda j: (0, j)),
        out_shape=jax.ShapeDtypeStruct((bsz, n), F32),
        name="ada_ln",
        compiler_params=_cparams("arbitrary"),
    )(c, w_ada, b_ada.reshape(1, n))


def _inproj_kernel(x_ref, sc_ref, sh_ref, g_ref, wdn_ref, wz_ref, wab_ref, wna_ref, alog_ref, dtb_ref,
                   dn_ref, z_ref, gate_ref, na_ref):
    h = _rms(x_ref[0]) * g_ref[...]
    h = h * (1.0 + sc_ref[0]) + sh_ref[0]
    hb = h.astype(BF16)
    dn_ref[0] = _dot(hb, wdn_ref[...]).astype(BF16)
    z_ref[0] = _dot(hb, wz_ref[...]).astype(BF16)
    na_ref[0] = _dot(hb, wna_ref[...]).astype(BF16)
    ab = _dot(hb, wab_ref[...])
    lane = lax.broadcasted_iota(jnp.int32, ab.shape, 1)
    decay = -jnp.exp(alog_ref[...]) * _softplus(ab + dtb_ref[...])
    gate_ref[0] = jnp.where(lane < 16, decay, _sigmoid(ab))


def _inproj(x, sc, sh, g, wdn, wz, wab, wna, alog, dtb, tm=512):
    bsz, s, d = x.shape
    ndn, nz, nna = wdn.shape[1], wz.shape[1], wna.shape[1]
    full = lambda a: pl.BlockSpec(a.shape, lambda b, i: (0,) * a.ndim)
    mod = pl.BlockSpec((1, 1, d), lambda b, i: (b, 0, 0))
    tok = lambda n: pl.BlockSpec((1, tm, n), lambda b, i: (b, i, 0))
    return pl.pallas_call(
        _inproj_kernel,
        grid=(bsz, s // tm),
        in_specs=[tok(d), mod, mod, full(g), full(wdn), full(wz), full(wab), full(wna), full(alog), full(dtb)],
        out_specs=[tok(ndn), tok(nz), tok(LANES), tok(nna)],
        out_shape=[jax.ShapeDtypeStruct((bsz, s, ndn), BF16), jax.ShapeDtypeStruct((bsz, s, nz), BF16),
                   jax.ShapeDtypeStruct((bsz, s, LANES), F32), jax.ShapeDtypeStruct((bsz, s, nna), BF16)],
        name="prenorm_inproj",
        compiler_params=_cparams("arbitrary", "arbitrary"),
    )(x, sc, sh, g, wdn, wz, wab, wna, alog, dtb)


def _conv_silu(x, w):
    s = x.shape[0]
    row = lax.broadcasted_iota(jnp.int32, x.shape, 0)
    pad = (DN_CONV - 1) // 2
    acc = x * w[pad:pad + 1]
    for j in range(DN_CONV):
        d = j - pad
        if d == 0:
            continue
        xs = pltpu.roll(x, (-d) % s, 0)
        ok = (row + d >= 0) & (row + d < s)
        acc = acc + jnp.where(ok, xs, 0.0) * w[j:j + 1]
    return _silu(acc)


def _chunk_cumsum(g, reverse):
    s = g.shape[0]
    pos = lax.broadcasted_iota(jnp.int32, g.shape, 0) & (DN_CHUNK - 1)
    sh = 1
    while sh < DN_CHUNK:
        if reverse:
            g = g + jnp.where(pos + sh < DN_CHUNK, pltpu.roll(g, s - sh, 0), 0.0)
        else:
            g = g + jnp.where(pos >= sh, pltpu.roll(g, sh, 0), 0.0)
        sh *= 2
    return g


def _delta_kernel(q_ref, k_ref, v_ref, wq_ref, wk_ref, wv_ref, gate_ref, gt_ref, z_ref, gon_ref, o_ref,
                  gcr_s, k_s, q_s, kb_s, kbg_s, qd_s, kt_s, vb_s, gci_s, egl_s, m_s, t0_s, t1_s, at_s, oacc_s):
    hp = pl.program_id(1)
    s = q_ref.shape[1]
    c = DN_CHUNK
    hd = HEAD_DIM
    n_chunks = s // c

    li, lj = _iota2((LANES, LANES))
    same_head_b = lax.shift_right_logical(li, 6) == lax.shift_right_logical(lj, 6)
    same_head = same_head_b.astype(F32)

    def l2n(x):
        return x * lax.rsqrt(_dot(x * x, same_head) + EPS)

    q2 = l2n(_conv_silu(q_ref[0].astype(F32), wq_ref[...])) * (hd ** -0.5)
    k2 = l2n(_conv_silu(k_ref[0].astype(F32), wk_ref[...]))
    v2 = _conv_silu(v_ref[0].astype(F32), wv_ref[...])
    k_s[...] = k2.astype(BF16)
    q_s[...] = q2.astype(BF16)

    gates = gate_ref[0]
    lane_s = lax.broadcasted_iota(jnp.int32, gates.shape, 1)

    def pair_cols(base):
        c0 = base + 2 * hp
        col0 = jnp.sum(jnp.where(lane_s == c0, gates, 0.0), axis=-1, keepdims=True)
        col1 = jnp.sum(jnp.where(lane_s == c0 + 1, gates, 0.0), axis=-1, keepdims=True)
        return jnp.where(lane_s < hd, col0, col1)

    ri, rj = _iota2((c, c))
    eye = (ri == rj).astype(F32)
    gt = gt_ref[0].reshape(n_chunks * 32, c)

    for direction in range(2):
        g2 = pair_cols(8 * direction)
        beta2 = pair_cols(16 + 8 * direction)
        gci2 = _chunk_cumsum(g2, reverse=direction == 1)
        gc3 = gci2.reshape(n_chunks, c, LANES)
        tot = gc3[:, c - 1:c, :] if direction == 0 else gc3[:, 0:1, :]
        gl2 = jnp.broadcast_to(tot, gc3.shape).reshape(s, LANES)
        egc2 = jnp.exp(gci2)
        kb2 = k2 * beta2
        kb_s[direction] = kb2.astype(BF16)
        kbg_s[direction] = (kb2 * egc2).astype(BF16)
        qd_s[direction] = (q2 * egc2).astype(BF16)
        kt_s[direction] = (k2 * jnp.exp(gl2 - gci2)).astype(BF16)
        vb_s[direction] = v2 * beta2
        gci_s[direction] = gci2
        egl_s[direction] = jnp.exp(gl2)
        cum_row = (ri <= rj) if direction == 0 else (ri >= rj)
        gcr_s[direction] = _dot(gt, cum_row.astype(F32), precision=HIGHEST).reshape(n_chunks, 32, c)

    lane_c = lax.broadcasted_iota(jnp.int32, (c, LANES), 1)
    head0 = lane_c < hd

    same = lambda sh: lax.shift_right_logical(ri, sh) == lax.shift_right_logical(rj, sh)
    group = 4
    n_levels = c.bit_length() - 2
    t_bufs = (t0_s, t1_s)

    def chunk_rows(g, j):
        return pl.ds(pl.multiple_of((g * group + j) * c, c), c)

    def chunk_matrices(g, carry):
        for j in range(group):
            rows = chunk_rows(g, j)
            kc = k_s[rows, :]
            qc = q_s[rows, :]
            zero = jnp.zeros_like(kc)
            for direction in range(2):
                incl = (ri >= rj) if direction == 0 else (ri <= rj)
                strict = (ri > rj) if direction == 0 else (ri < rj)
                kbc = kb_s[direction, rows, :]
                gci = gci_s[direction, rows, :]
                for hh in range(2):
                    hm = head0 if hh == 0 else jnp.logical_not(head0)
                    gi = jnp.broadcast_to(gci[:, hh * hd:hh * hd + 1], (c, c))
                    gj = jnp.broadcast_to(
                        gcr_s[direction, g * group + j, pl.ds(8 * direction + 2 * hp + hh, 1), :], (c, c))
                    decay = jnp.where(incl, jnp.exp(jnp.where(incl, gi - gj, 0.0)), 0.0)
                    m = jnp.where(strict, _dot_nt(jnp.where(hm, kbc, zero), kc) * decay, 0.0)
                    m_s[direction, hh, rows, :] = m.astype(BF16)
                    t_bufs[n_levels % 2][direction, hh, rows, :] =(eye - jnp.where(same(1), m, 0.0)).astype(BF16)
                    at_s[direction, hh, rows, :] = (_dot_nt(jnp.where(hm, qc, zero), kc) * decay).astype(BF16)
        return carry

    lax.fori_loop(0, n_chunks // group, chunk_matrices, 0)

    for level in range(n_levels):
        def merge_level(g, carry, level=level):
            sh = level + 1
            src = (n_levels - level) % 2
            cmask = same(sh + 1) & jnp.logical_not(same(sh))
            for j in range(group):
                rows = chunk_rows(g, j)
                for direction in range(2):
                    for hh in range(2):
                        tb = t_bufs[src][direction, hh, rows, :]
                        mb = m_s[direction, hh, rows, :]
                        cm = jnp.where(cmask, mb, jnp.zeros_like(mb))
                        y = _dot(tb, _dot(cm, tb).astype(BF16))
                        t_bufs[1 - src][direction, hh, rows, :] =jnp.where(cmask, (-y).astype(BF16), tb)
            return carry

        lax.fori_loop(0, n_chunks // group, merge_level, 0)

    oacc_s[...] = jnp.zeros_like(oacc_s)

    def scan_step(i, states):
        new_states = []
        for direction in range(2):
            n = i if direction == 0 else n_chunks - 1 - i
            r0 = pl.multiple_of(n * c, c)
            rows = pl.ds(r0, c)
            sb = states[direction].astype(BF16)
            rhs = (vb_s[direction, rows, :] - _dot(kbg_s[direction, rows, :], sb)).astype(BF16)
            v_new = jnp.where(head0, _dot(t0_s[direction, 0, rows, :], rhs),
                              _dot(t0_s[direction, 1, rows, :], rhs)).astype(BF16)
            intra = jnp.where(head0, _dot(at_s[direction, 0, rows, :], v_new),
                              _dot(at_s[direction, 1, rows, :], v_new))
            oacc_s[rows, :] += _dot(qd_s[direction, rows, :], sb) + intra
            upd = _dot_tn(kt_s[direction, rows, :], v_new)
            new_states.append(states[direction] * egl_s[direction, pl.ds(r0, 1), :]
                              + jnp.where(same_head_b, upd, 0.0))
        return tuple(new_states)

    zero_state = jnp.zeros((LANES, LANES), F32)
    lax.fori_loop(0, n_chunks, scan_step, (zero_state, zero_state))

    o2 = oacc_s[...]
    ms = _dot(o2 * o2, same_head) * (1.0 / hd)
    o2 = o2 * lax.rsqrt(ms + EPS) * gon_ref[...]
    o_ref[0] = (o2 * _silu(z_ref[0].astype(F32))).astype(BF16)


def _delta(dn, w_conv, gates, gt, z, gon2):
    bsz, s, _ = dn.shape
    n_pairs = z.shape[2] // LANES
    c = DN_CHUNK
    col = lambda off: pl.BlockSpec((1, s, LANES), lambda b, p: (b, 0, off + p))
    wcol = lambda off: pl.BlockSpec((DN_CONV, LANES), lambda b, p: (0, off + p))
    two = lambda dt: pltpu.VMEM((2, s, LANES), dt)
    return pl.pallas_call(
        _delta_kernel,
        grid=(bsz, n_pairs),
        in_specs=[col(0), col(n_pairs), col(2 * n_pairs), wcol(0), wcol(n_pairs), wcol(2 * n_pairs),
                  pl.BlockSpec((1, s, LANES), lambda b, p: (b, 0, 0)),
                  pl.BlockSpec((1, s // c, 32, c), lambda b, p: (b, 0, 0, 0)),
                  col(0),
                  pl.BlockSpec((1, LANES), lambda b, p: (0, 0))],
        out_specs=col(0),
        out_shape=jax.ShapeDtypeStruct((bsz, s, n_pairs * LANES), BF16),
        scratch_shapes=[pltpu.VMEM((2, s // c, 32, c), F32),
                        pltpu.VMEM((s, LANES), BF16), pltpu.VMEM((s, LANES), BF16),
                        two(BF16), two(BF16), two(BF16), two(BF16),
                        two(F32), two(F32), two(F32),
                        pltpu.VMEM((2, 2, s, LANES), BF16), pltpu.VMEM((2, 2, s, LANES), BF16),
                        pltpu.VMEM((2, 2, s, LANES), BF16), pltpu.VMEM((2, 2, s, LANES), BF16),
                        pltpu.VMEM((s, LANES), F32)],
        name="delta_rule",
        compiler_params=_cparams("arbitrary", "arbitrary"),
    )(dn, dn, dn, w_conv, w_conv, w_conv, gates, gt, z, gon2)


def _na_bias_tables(rpb):
    n_heads = rpb.shape[0]
    cc = np.arange(GRID_W)
    c0 = np.clip(cc - NA_WIN_W // 2, 0, GRID_W - NA_WIN_W)
    kc = np.arange(GRID_W)
    in_win = (kc[None, :] >= c0[:, None]) & (kc[None, :] < c0[:, None] + NA_WIN_W)
    col_off = kc[None, :] - cc[:, None] + NA_WIN_W - 1
    onehot = (col_off[None] == np.arange(2 * NA_WIN_W - 1)[:, None, None]) & in_win[None]
    rows = jnp.stack([rpb[:, NA_WIN_H - 1 - d:2 * NA_WIN_H - 1 - d, :] for d in range(NA_WIN_H)], axis=1)
    rows = rows.reshape(n_heads // 2, 2, NA_WIN_H, NA_WIN_H, 2 * NA_WIN_W - 1).astype(F32)
    bias = jnp.einsum("phdic,cqk->pdhqik", rows, jnp.asarray(onehot, F32), precision=HIGHEST)
    bias = bias + jnp.asarray(np.where(in_win, 0.0, NEG_BIG), F32)[None, None, None, :, None, :]
    return bias.reshape(n_heads // 2, NA_WIN_H, 2 * GRID_W, NA_WIN_H * GRID_W)


def _natten_kernel(q_ref, k_ref, v_ref, bias_ref, o_ref):
    s = q_ref.shape[1]
    rows = s // GRID_W
    w = GRID_W
    win = min(NA_WIN_H, rows)
    hd = HEAD_DIM
    lane = lax.broadcasted_iota(jnp.int32, (w, LANES), 1)
    first = lane < hd
    for r in range(rows):
        r0 = min(max(r - win // 2, 0), rows - win)
        d = r - r0
        q = q_ref[0, r * w:(r + 1) * w, :]
        zero = jnp.zeros_like(q)
        q_st = jnp.concatenate([jnp.where(first, q, zero), jnp.where(first, zero, q)], axis=0)
        kw = k_ref[0, r0 * w:(r0 + win) * w, :]
        vw = v_ref[0, r0 * w:(r0 + win) * w, :]
        sc = _dot_nt(q_st, kw) + bias_ref[0, d]
        sc = sc - jnp.max(sc, axis=-1, keepdims=True)
        p = jnp.exp(sc)
        denom = jnp.sum(p, axis=-1, keepdims=True)
        o = _dot(p.astype(BF16), vw) / denom
        o_ref[0, r * w:(r + 1) * w, :] = jnp.where(first, o[:w], o[w:]).astype(BF16)


def _natten(na, bias):
    bsz, s, n3 = na.shape
    n_pairs = n3 // (3 * LANES)
    col = lambda off: pl.BlockSpec((1, s, LANES), lambda p, b: (b, 0, off + p))
    return pl.pallas_call(
        _natten_kernel,
        grid=(n_pairs, bsz),
        in_specs=[col(0), col(n_pairs), col(2 * n_pairs),
                  pl.BlockSpec((1,) + bias.shape[1:], lambda p, b: (p, 0, 0, 0))],
        out_specs=col(0),
        out_shape=jax.ShapeDtypeStruct((bsz, s, n_pairs * LANES), BF16),
        name="natten",
        compiler_params=_cparams("arbitrary", "arbitrary"),
    )(na, na, na, bias)


def _outproj_kernel(odn_ref, ona_ref, x_ref, wo_dn_ref, wo_na_ref, gpost_ref, gt1_ref, gpre_ref, sc_ref, sh_ref,
                    wr_ref, br_ref, x1_ref, h_ref, comb_ref):
    y = _dot(odn_ref[0], wo_dn_ref[...]) + _dot(ona_ref[0], wo_na_ref[...])
    x1 = x_ref[0] + gt1_ref[0] * (_rms(y) * gpost_ref[...])
    x1_ref[0] = x1
    h = _rms(x1) * gpre_ref[...]
    h = h * (1.0 + sc_ref[0]) + sh_ref[0]
    h_ref[0] = h.astype(BF16)

    logits = _dot(h, wr_ref[...], precision=HIGHEST) + br_ref[...]
    lane = lax.broadcasted_iota(jnp.int32, logits.shape, 1).astype(F32)
    big = float(LANES)

    def masked_softmax(mask):
        lm = jnp.where(mask, logits, NEG_BIG)
        e = jnp.where(mask, jnp.exp(lm - jnp.max(lm, axis=-1, keepdims=True)), 0.0)
        return e / jnp.sum(e, axis=-1, keepdims=True)

    def top1(p, mask):
        pm = jnp.where(mask, p, -1.0)
        best = jnp.max(pm, axis=-1, keepdims=True)
        idx = jnp.min(jnp.where(mask & (pm == best), lane, big), axis=-1, keepdims=True)
        return best, idx

    gmask = lane < N_GROUPS
    gp_top, g_idx = top1(masked_softmax(gmask), gmask)
    e_lo = N_GROUPS + g_idx * EXPERTS_PER_GROUP
    emask = (lane >= e_lo) & (lane < e_lo + EXPERTS_PER_GROUP)
    pe = masked_softmax(emask)
    p1, i1 = top1(pe, emask)
    p2, i2 = top1(pe, emask & (lane != i1))
    w1 = gp_top * p1 / (p1 + p2)
    w2 = gp_top * p2 / (p1 + p2)
    el = lane + N_GROUPS
    comb_ref[0] = jnp.where(el == i1, w1, 0.0) + jnp.where(el == i2, w2, 0.0)


def _outproj(odn, ona, x, wo_dn, wo_na, gpost, gt1, gpre, sc2, sh2, wr, br, tm=512):
    bsz, s, d = x.shape
    full = lambda a: pl.BlockSpec(a.shape, lambda b, i: (0,) * a.ndim)
    mod = pl.BlockSpec((1, 1, d), lambda b, i: (b, 0, 0))
    tok = lambda n: pl.BlockSpec((1, tm, n), lambda b, i: (b, i, 0))
    return pl.pallas_call(
        _outproj_kernel,
        grid=(bsz, s // tm),
        in_specs=[tok(odn.shape[2]), tok(ona.shape[2]), tok(d), full(wo_dn), full(wo_na), full(gpost), mod,
                  full(gpre), mod, mod, full(wr), full(br)],
        out_specs=[tok(d), tok(d), tok(LANES)],
        out_shape=[jax.ShapeDtypeStruct((bsz, s, d), F32), jax.ShapeDtypeStruct((bsz, s, d), BF16),
                   jax.ShapeDtypeStruct((bsz, s, LANES), F32)],
        name="outproj_route",
        compiler_params=_cparams("arbitrary", "arbitrary"),
    )(odn, ona, x, wo_dn, wo_na, gpost, gt1, gpre, sc2, sh2, wr, br)


def _moe_kernel(h_ref, comb_ref, wg_ref, wu_ref, wd_ref, x1_ref, gpost_ref, gt2_ref, o_ref, acc_ref):
    e = pl.program_id(2)

    @pl.when(e == 0)
    def _():
        acc_ref[...] = jnp.zeros_like(acc_ref)

    h = h_ref[0]
    gate = _dot(h, wg_ref[0].astype(BF16))
    up = _dot(h, wu_ref[0].astype(BF16))
    he = (_silu(gate) * up).astype(BF16)
    out = _dot(he, wd_ref[0].astype(BF16))
    comb = comb_ref[0]
    lane = lax.broadcasted_iota(jnp.int32, comb.shape, 1)
    w = jnp.sum(jnp.where(lane == e, comb, 0.0), axis=-1, keepdims=True)
    acc_ref[...] += w * out

    @pl.when(e == pl.num_programs(2) - 1)
    def _():
        o_ref[0] = x1_ref[0] + gt2_ref[0] * (_rms(acc_ref[...]) * gpost_ref[...])


def _moe(h, comb, w_gate, w_up, w_down, x1, gpost, gt2, tm=1024):
    bsz, s, d = x1.shape
    n_exp, _, de = w_gate.shape
    tok = lambda n: pl.BlockSpec((1, tm, n), lambda b, i, e: (b, i, 0))
    return pl.pallas_call(
        _moe_kernel,
        grid=(bsz, s // tm, n_exp),
        in_specs=[tok(d), tok(LANES),
                  pl.BlockSpec((1, d, de), lambda b, i, e: (e, 0, 0)),
                  pl.BlockSpec((1, d, de), lambda b, i, e: (e, 0, 0)),
                  pl.BlockSpec((1, de, d), lambda b, i, e: (e, 0, 0)),
                  tok(d),
                  pl.BlockSpec((1, d), lambda b, i, e: (0, 0)),
                  pl.BlockSpec((1, 1, d), lambda b, i, e: (b, 0, 0))],
        out_specs=tok(d),
        out_shape=jax.ShapeDtypeStruct((bsz, s, d), F32),
        scratch_shapes=[pltpu.VMEM((tm, d), F32)],
        name="moe_dense",
        compiler_params=_cparams("arbitrary", "arbitrary", "arbitrary"),
    )(h, comb, w_gate, w_up, w_down, x1, gpost, gt2)


def _pad_lanes(a, n=LANES):
    return jnp.pad(a, [(0, 0)] * (a.ndim - 1) + [(0, n - a.shape[-1])])


def kernel(x, c, w_ada, b_ada, g_pre_mix, g_post_mix, w_in, w_conv_dn, a_log_dn, dt_bias_dn, g_onorm_dn, rpb_na,
           w_out, g_pre_ffn, g_post_ffn, w_group, b_group, w_expert, b_expert, w_gate, w_up, w_down):
    bsz, s, d = x.shape
    depth = w_ada.shape[0]
    n_dn = a_log_dn.shape[2]
    dn_w = n_dn * HEAD_DIM
    for l in range(depth):
        ada = _ada(c, w_ada[l], b_ada[l])
        sh1, sc1, gt1, sh2, sc2, gt2 = [a.reshape(bsz, 1, d) for a in jnp.split(ada, 6, axis=-1)]

        wi = w_in[l]
        o_z, o_ab, o_na = 3 * dn_w, 4 * dn_w, 4 * dn_w + 4 * n_dn
        wdn = wi[:, :o_z].astype(BF16)
        wz = wi[:, o_z:o_ab].astype(BF16)
        wab = _pad_lanes(wi[:, o_ab:o_na]).astype(BF16)
        na_w = (wi.shape[1] - o_na) // 3
        wna = jnp.concatenate([wi[:, o_na:o_na + na_w] * (HEAD_DIM ** -0.5), wi[:, o_na + na_w:]], axis=1).astype(BF16)
        alog = _pad_lanes(a_log_dn[l].reshape(1, 2 * n_dn))
        dtb = _pad_lanes(dt_bias_dn[l].reshape(1, 2 * n_dn))
        dn, z, gates, na = _inproj(x, sc1, sh1, g_pre_mix[l].reshape(1, d), wdn, wz, wab, wna, alog, dtb)

        gt = gates[:, :, :32].reshape(bsz, s // DN_CHUNK, DN_CHUNK, 32).transpose(0, 1, 3, 2)
        gon2 = jnp.tile(g_onorm_dn[l].reshape(1, HEAD_DIM), (1, LANES // HEAD_DIM))
        o_dn = _delta(dn, w_conv_dn[l], gates, gt, z, gon2)

        o_na = _natten(na, _na_bias_tables(rpb_na[l]))

        wo = w_out[l].astype(BF16)
        wr = _pad_lanes(jnp.concatenate([w_group[l], w_expert[l]], axis=1))
        br = _pad_lanes(jnp.concatenate([b_group[l], b_expert[l]]).reshape(1, -1))
        x1, h2, comb = _outproj(o_dn, o_na, x, wo[:dn_w], wo[dn_w:], g_post_mix[l].reshape(1, d), gt1,
                                g_pre_ffn[l].reshape(1, d), sc2, sh2, wr, br)
        x = _moe(h2, comb, w_gate[l], w_up[l], w_down[l], x1, g_post_ffn[l].reshape(1, d), gt2)
    return x
```

```python
import numpy as np
import jax
import jax.numpy as jnp
from jax import lax
from jax.experimental import pallas as pl
from jax.experimental.pallas import tpu as pltpu

F32 = jnp.float32
BF16 = jnp.bfloat16
HIGHEST = lax.Precision.HIGHEST

GRID_W = 64
HEAD_DIM = 64
DN_CONV = 5
DN_CHUNK = 128
NA_WIN_H = 8
NA_WIN_W = 16
N_GROUPS = 4
EXPERTS_PER_GROUP = 8
N_EXPERTS = N_GROUPS * EXPERTS_PER_GROUP
EPS = 1e-6
LANES = 128
NEG_BIG = -1e30
VMEM_LIMIT = 56 * 1024 * 1024
MOE_ROW_TILE = 256


def _cparams(*sem):
    return pltpu.CompilerParams(dimension_semantics=sem, vmem_limit_bytes=VMEM_LIMIT)


def _sigmoid(x):
    return 1.0 / (1.0 + jnp.exp(-x))


def _silu(x):
    return x * _sigmoid(x)


def _softplus(x):
    return jnp.maximum(x, 0.0) + jnp.log1p(jnp.exp(-jnp.abs(x)))


def _dot(a, b, **kw):
    return jnp.dot(a, b, preferred_element_type=F32, **kw)


def _dot_nt(a, b, **kw):
    return lax.dot_general(a, b, (((1,), (1,)), ((), ())), preferred_element_type=F32, **kw)


def _dot_tn(a, b, **kw):
    return lax.dot_general(a, b, (((0,), (0,)), ((), ())), preferred_element_type=F32, **kw)


def _rms(x):
    return x * lax.rsqrt(jnp.mean(x * x, axis=-1, keepdims=True) + EPS)


def _iota2(shape):
    return lax.broadcasted_iota(jnp.int32, shape, 0), lax.broadcasted_iota(jnp.int32, shape, 1)


def _ada_kernel(c_ref, w_ref, b_ref, o_ref):
    cond = _silu(c_ref[...])
    o_ref[...] = _dot(cond, w_ref[...], precision=HIGHEST) + b_ref[...]


def _ada(c, w_ada, b_ada):
    bsz, d = c.shape
    n = w_ada.shape[1]
    tn = 1536
    return pl.pallas_call(
        _ada_kernel,
        grid=(n // tn,),
        in_specs=[pl.BlockSpec((bsz, d), lambda j: (0, 0)),
                  pl.BlockSpec((d, tn), lambda j: (0, j)),
                  pl.BlockSpec((1, tn), lambda j: (0, j))],
        out_specs=pl.BlockSpec((bsz, tn), lambda j: (0, j)),
        out_shape=jax.ShapeDtypeStruct((bsz, n), F32),
        name="ada_ln",
        compiler_params=_cparams("arbitrary"),
    )(c, w_ada, b_ada.reshape(1, n))


def _inproj_kernel(x_ref, sc_ref, sh_ref, g_ref, wdn_ref, wz_ref, wab_ref, wna_ref, alog_ref, dtb_ref,
                   dn_ref, z_ref, gate_ref, na_ref):
    h = _rms(x_ref[0]) * g_ref[...]
    h = h * (1.0 + sc_ref[0]) + sh_ref[0]
    hb = h.astype(BF16)
    dn_ref[0] = _dot(hb, wdn_ref[...]).astype(BF16)
    z_ref[0] = _dot(hb, wz_ref[...]).astype(BF16)
    na_ref[0] = _dot(hb, wna_ref[...]).astype(BF16)
    ab = _dot(hb, wab_ref[...])
    lane = lax.broadcasted_iota(jnp.int32, ab.shape, 1)
    decay = -jnp.exp(alog_ref[...]) * _softplus(ab + dtb_ref[...])
    gate_ref[0] = jnp.where(lane < 16, decay, _sigmoid(ab))


def _inproj(x, sc, sh, g, wdn, wz, wab, wna, alog, dtb, tm=512):
    bsz, s, d = x.shape
    ndn, nz, nna = wdn.shape[1], wz.shape[1], wna.shape[1]
    full = lambda a: pl.BlockSpec(a.shape, lambda b, i: (0,) * a.ndim)
    mod = pl.BlockSpec((1, 1, d), lambda b, i: (b, 0, 0))
    tok = lambda n: pl.BlockSpec((1, tm, n), lambda b, i: (b, i, 0))
    return pl.pallas_call(
        _inproj_kernel,
        grid=(bsz, s // tm),
        in_specs=[tok(d), mod, mod, full(g), full(wdn), full(wz), full(wab), full(wna), full(alog), full(dtb)],
        out_specs=[tok(ndn), tok(nz), tok(LANES), tok(nna)],
        out_shape=[jax.ShapeDtypeStruct((bsz, s, ndn), BF16), jax.ShapeDtypeStruct((bsz, s, nz), BF16),
                   jax.ShapeDtypeStruct((bsz, s, LANES), F32), jax.ShapeDtypeStruct((bsz, s, nna), BF16)],
        name="prenorm_inproj",
        compiler_params=_cparams("arbitrary", "arbitrary"),
    )(x, sc, sh, g, wdn, wz, wab, wna, alog, dtb)


def _conv_silu(x, w):
    s = x.shape[0]
    row = lax.broadcasted_iota(jnp.int32, x.shape, 0)
    pad = (DN_CONV - 1) // 2
    acc = x * w[pad:pad + 1]
    for j in range(DN_CONV):
        d = j - pad
        if d == 0:
            continue
        xs = pltpu.roll(x, (-d) % s, 0)
        ok = (row + d >= 0) & (row + d < s)
        acc = acc + jnp.where(ok, xs, 0.0) * w[j:j + 1]
    return _silu(acc)


def _chunk_cumsum(g, reverse):
    s = g.shape[0]
    pos = lax.broadcasted_iota(jnp.int32, g.shape, 0) & (DN_CHUNK - 1)
    sh = 1
    while sh < DN_CHUNK:
        if reverse:
            g = g + jnp.where(pos + sh < DN_CHUNK, pltpu.roll(g, s - sh, 0), 0.0)
        else:
            g = g + jnp.where(pos >= sh, pltpu.roll(g, sh, 0), 0.0)
        sh *= 2
    return g


def _delta_kernel(q_ref, k_ref, v_ref, wq_ref, wk_ref, wv_ref, gate_ref, gt_ref, z_ref, gon_ref, o_ref,
                  gcr_s, k_s, q_s, kb_s, kbg_s, qd_s, kt_s, vb_s, gci_s, egl_s, m_s, t0_s, t1_s, at_s, oacc_s):
    hp = pl.program_id(1)
    s = q_ref.shape[1]
    c = DN_CHUNK
    hd = HEAD_DIM
    n_chunks = s // c

    li, lj = _iota2((LANES, LANES))
    same_head_b = lax.shift_right_logical(li, 6) == lax.shift_right_logical(lj, 6)
    same_head = same_head_b.astype(F32)

    def l2n(x):
        return x * lax.rsqrt(_dot(x * x, same_head) + EPS)

    q2 = l2n(_conv_silu(q_ref[0].astype(F32), wq_ref[...])) * (hd ** -0.5)
    k2 = l2n(_conv_silu(k_ref[0].astype(F32), wk_ref[...]))
    v2 = _conv_silu(v_ref[0].astype(F32), wv_ref[...])
    k_s[...] = k2.astype(BF16)
    q_s[...] = q2.astype(BF16)

    gates = gate_ref[0]
    lane_s = lax.broadcasted_iota(jnp.int32, gates.shape, 1)

    def pair_cols(base):
        c0 = base + 2 * hp
        col0 = jnp.sum(jnp.where(lane_s == c0, gates, 0.0), axis=-1, keepdims=True)
        col1 = jnp.sum(jnp.where(lane_s == c0 + 1, gates, 0.0), axis=-1, keepdims=True)
        return jnp.where(lane_s < hd, col0, col1)

    ri, rj = _iota2((c, c))
    eye = (ri == rj).astype(F32)
    gt = gt_ref[0].reshape(n_chunks * 32, c)

    for direction in range(2):
        g2 = pair_cols(8 * direction)
        beta2 = pair_cols(16 + 8 * direction)
        gci2 = _chunk_cumsum(g2, reverse=direction == 1)
        gc3 = gci2.reshape(n_chunks, c, LANES)
        tot = gc3[:, c - 1:c, :] if direction == 0 else gc3[:, 0:1, :]
        gl2 = jnp.broadcast_to(tot, gc3.shape).reshape(s, LANES)
        egc2 = jnp.exp(gci2)
        kb2 = k2 * beta2
        kb_s[direction] = kb2.astype(BF16)
        kbg_s[direction] = (kb2 * egc2).astype(BF16)
        qd_s[direction] = (q2 * egc2).astype(BF16)
        kt_s[direction] = (k2 * jnp.exp(gl2 - gci2)).astype(BF16)
        vb_s[direction] = v2 * beta2
        gci_s[direction] = gci2
        egl_s[direction] = jnp.exp(gl2)
        cum_row = (ri <= rj) if direction == 0 else (ri >= rj)
        gcr_s[direction] = _dot(gt, cum_row.astype(F32), precision=HIGHEST).reshape(n_chunks, 32, c)

    lane_c = lax.broadcasted_iota(jnp.int32, (c, LANES), 1)
    head0 = lane_c < hd

    same = lambda sh: lax.shift_right_logical(ri, sh) == lax.shift_right_logical(rj, sh)
    group = 4
    n_levels = c.bit_length() - 2
    t_bufs = (t0_s, t1_s)

    def chunk_rows(g, j):
        return pl.ds(pl.multiple_of((g * group + j) * c, c), c)

    def chunk_matrices(g, carry):
        for j in range(group):
            rows = chunk_rows(g, j)
            kc = k_s[rows, :]
            qc = q_s[rows, :]
            zero = jnp.zeros_like(kc)
            for direction in range(2):
                incl = (ri >= rj) if direction == 0 else (ri <= rj)
                strict = (ri > rj) if direction == 0 else (ri < rj)
                kbc = kb_s[direction, rows, :]
                gci = gci_s[direction, rows, :]
                for hh in range(2):
                    hm = head0 if hh == 0 else jnp.logical_not(head0)
                    gi = jnp.broadcast_to(gci[:, hh * hd:hh * hd + 1], (c, c))
                    gj = jnp.broadcast_to(
                        gcr_s[direction, g * group + j, pl.ds(8 * direction + 2 * hp + hh, 1), :], (c, c))
                    decay = jnp.where(incl, jnp.exp(jnp.where(incl, gi - gj, 0.0)), 0.0)
                    m = jnp.where(strict, _dot_nt(jnp.where(hm, kbc, zero), kc) * decay, 0.0)
                    m_s[direction, hh, rows, :] = m.astype(BF16)
                    t_bufs[n_levels % 2][direction, hh, rows, :] =(eye - jnp.where(same(1), m, 0.0)).astype(BF16)
                    at_s[direction, hh, rows, :] = (_dot_nt(jnp.where(hm, qc, zero), kc) * decay).astype(BF16)
        return carry

    lax.fori_loop(0, n_chunks // group, chunk_matrices, 0)

    for level in range(n_levels):
        def merge_level(g, carry, level=level):
            sh = level + 1
            src = (n_levels - level) % 2
            cmask = same(sh + 1) & jnp.logical_not(same(sh))
            for j in range(group):
                rows = chunk_rows(g, j)
                for direction in range(2):
                    for hh in range(2):
                        tb = t_bufs[src][direction, hh, rows, :]
                        mb = m_s[direction, hh, rows, :]
                        cm = jnp.where(cmask, mb, jnp.zeros_like(mb))
                        y = _dot(tb, _dot(cm, tb).astype(BF16))
                        t_bufs[1 - src][direction, hh, rows, :] =jnp.where(cmask, (-y).astype(BF16), tb)
            return carry

        lax.fori_loop(0, n_chunks // group, merge_level, 0)

    oacc_s[...] = jnp.zeros_like(oacc_s)

    def scan_step(i, states):
        new_states = []
        for direction in range(2):
            n = i if direction == 0 else n_chunks - 1 - i
            r0 = pl.multiple_of(n * c, c)
            rows = pl.ds(r0, c)
            sb = states[direction].astype(BF16)
            rhs = (vb_s[direction, rows, :] - _dot(kbg_s[direction, rows, :], sb)).astype(BF16)
            v_new = jnp.where(head0, _dot(t0_s[direction, 0, rows, :], rhs),
                              _dot(t0_s[direction, 1, rows, :], rhs)).astype(BF16)
            intra = jnp.where(head0, _dot(at_s[direction, 0, rows, :], v_new),
                              _dot(at_s[direction, 1, rows, :], v_new))
            oacc_s[rows, :] += _dot(qd_s[direction, rows, :], sb) + intra
            upd = _dot_tn(kt_s[direction, rows, :], v_new)
            new_states.append(states[direction] * egl_s[direction, pl.ds(r0, 1), :]
                              + jnp.where(same_head_b, upd, 0.0))
        return tuple(new_states)

    zero_state = jnp.zeros((LANES, LANES), F32)
    lax.fori_loop(0, n_chunks, scan_step, (zero_state, zero_state))

    o2 = oacc_s[...]
    ms = _dot(o2 * o2, same_head) * (1.0 / hd)
    o2 = o2 * lax.rsqrt(ms + EPS) * gon_ref[...]
    o_ref[0] = (o2 * _silu(z_ref[0].astype(F32))).astype(BF16)


def _delta(dn, w_conv, gates, gt, z, gon2):
    bsz, s, _ = dn.shape
    n_pairs = z.shape[2] // LANES
    c = DN_CHUNK
    col = lambda off: pl.BlockSpec((1, s, LANES), lambda b, p: (b, 0, off + p))
    wcol = lambda off: pl.BlockSpec((DN_CONV, LANES), lambda b, p: (0, off + p))
    two = lambda dt: pltpu.VMEM((2, s, LANES), dt)
    return pl.pallas_call(
        _delta_kernel,
        grid=(bsz, n_pairs),
        in_specs=[col(0), col(n_pairs), col(2 * n_pairs), wcol(0), wcol(n_pairs), wcol(2 * n_pairs),
                  pl.BlockSpec((1, s, LANES), lambda b, p: (b, 0, 0)),
                  pl.BlockSpec((1, s // c, 32, c), lambda b, p: (b, 0, 0, 0)),
                  col(0),
                  pl.BlockSpec((1, LANES), lambda b, p: (0, 0))],
        out_specs=col(0),
        out_shape=jax.ShapeDtypeStruct((bsz, s, n_pairs * LANES), BF16),
        scratch_shapes=[pltpu.VMEM((2, s // c, 32, c), F32),
                        pltpu.VMEM((s, LANES), BF16), pltpu.VMEM((s, LANES), BF16),
                        two(BF16), two(BF16), two(BF16), two(BF16),
                        two(F32), two(F32), two(F32),
                        pltpu.VMEM((2, 2, s, LANES), BF16), pltpu.VMEM((2, 2, s, LANES), BF16),
                        pltpu.VMEM((2, 2, s, LANES), BF16), pltpu.VMEM((2, 2, s, LANES), BF16),
                        pltpu.VMEM((s, LANES), F32)],
        name="delta_rule",
        compiler_params=_cparams("arbitrary", "arbitrary"),
    )(dn, dn, dn, w_conv, w_conv, w_conv, gates, gt, z, gon2)


def _na_bias_tables(rpb):
    n_heads = rpb.shape[0]
    cc = np.arange(GRID_W)
    c0 = np.clip(cc - NA_WIN_W // 2, 0, GRID_W - NA_WIN_W)
    kc = np.arange(GRID_W)
    in_win = (kc[None, :] >= c0[:, None]) & (kc[None, :] < c0[:, None] + NA_WIN_W)
    col_off = kc[None, :] - cc[:, None] + NA_WIN_W - 1
    onehot = (col_off[None] == np.arange(2 * NA_WIN_W - 1)[:, None, None]) & in_win[None]
    rows = jnp.stack([rpb[:, NA_WIN_H - 1 - d:2 * NA_WIN_H - 1 - d, :] for d in range(NA_WIN_H)], axis=1)
    rows = rows.reshape(n_heads // 2, 2, NA_WIN_H, NA_WIN_H, 2 * NA_WIN_W - 1).astype(F32)
    bias = jnp.einsum("phdic,cqk->pdhqik", rows, jnp.asarray(onehot, F32), precision=HIGHEST)
    bias = bias + jnp.asarray(np.where(in_win, 0.0, NEG_BIG), F32)[None, None, None, :, None, :]
    return bias.reshape(n_heads // 2, NA_WIN_H, 2 * GRID_W, NA_WIN_H * GRID_W)


def _natten_kernel(q_ref, k_ref, v_ref, bias_ref, o_ref):
    s = q_ref.shape[1]
    rows = s // GRID_W
    w = GRID_W
    win = min(NA_WIN_H, rows)
    hd = HEAD_DIM
    lane = lax.broadcasted_iota(jnp.int32, (w, LANES), 1)
    first = lane < hd
    for r in range(rows):
        r0 = min(max(r - win // 2, 0), rows - win)
        d = r - r0
        q = q_ref[0, r * w:(r + 1) * w, :]
        zero = jnp.zeros_like(q)
        q_st = jnp.concatenate([jnp.where(first, q, zero), jnp.where(first, zero, q)], axis=0)
        kw = k_ref[0, r0 * w:(r0 + win) * w, :]
        vw = v_ref[0, r0 * w:(r0 + win) * w, :]
        sc = _dot_nt(q_st, kw) + bias_ref[0, d]
        sc = sc - jnp.max(sc, axis=-1, keepdims=True)
        p = jnp.exp(sc)
        denom = jnp.sum(p, axis=-1, keepdims=True)
        o = _dot(p.astype(BF16), vw) / denom
        o_ref[0, r * w:(r + 1) * w, :] = jnp.where(first, o[:w], o[w:]).astype(BF16)


def _natten(na, bias):
    bsz, s, n3 = na.shape
    n_pairs = n3 // (3 * LANES)
    col = lambda off: pl.BlockSpec((1, s, LANES), lambda p, b: (b, 0, off + p))
    return pl.pallas_call(
        _natten_kernel,
        grid=(n_pairs, bsz),
        in_specs=[col(0), col(n_pairs), col(2 * n_pairs),
                  pl.BlockSpec((1,) + bias.shape[1:], lambda p, b: (p, 0, 0, 0))],
        out_specs=col(0),
        out_shape=jax.ShapeDtypeStruct((bsz, s, n_pairs * LANES), BF16),
        name="natten",
        compiler_params=_cparams("arbitrary", "arbitrary"),
    )(na, na, na, bias)


def _outproj_kernel(odn_ref, ona_ref, x_ref, wo_dn_ref, wo_na_ref, gpost_ref, gt1_ref, gpre_ref, sc_ref, sh_ref,
                    wr_ref, br_ref, x1_ref, h_ref, route_ref, cnt_ref, cnt_s):
    y = _dot(odn_ref[0], wo_dn_ref[...]) + _dot(ona_ref[0], wo_na_ref[...])
    x1 = x_ref[0] + gt1_ref[0] * (_rms(y) * gpost_ref[...])
    x1_ref[0] = x1
    h = _rms(x1) * gpre_ref[...]
    h = h * (1.0 + sc_ref[0]) + sh_ref[0]
    h_ref[0] = h

    logits = _dot(h, wr_ref[...], precision=HIGHEST) + br_ref[...]
    lane = lax.broadcasted_iota(jnp.int32, logits.shape, 1).astype(F32)
    big = float(LANES)

    def masked_softmax(mask):
        lm = jnp.where(mask, logits, NEG_BIG)
        e = jnp.where(mask, jnp.exp(lm - jnp.max(lm, axis=-1, keepdims=True)), 0.0)
        return e / jnp.sum(e, axis=-1, keepdims=True)

    def top1(p, mask):
        pm = jnp.where(mask, p, -1.0)
        best = jnp.max(pm, axis=-1, keepdims=True)
        idx = jnp.min(jnp.where(mask & (pm == best), lane, big), axis=-1, keepdims=True)
        return best, idx

    gmask = lane < N_GROUPS
    gp_top, g_idx = top1(masked_softmax(gmask), gmask)
    e_lo = N_GROUPS + g_idx * EXPERTS_PER_GROUP
    emask = (lane >= e_lo) & (lane < e_lo + EXPERTS_PER_GROUP)
    pe = masked_softmax(emask)
    p1, i1 = top1(pe, emask)
    p2, i2 = top1(pe, emask & (lane != i1))
    w1 = gp_top * p1 / (p1 + p2)
    w2 = gp_top * p2 / (p1 + p2)

    @pl.when((pl.program_id(0) == 0) & (pl.program_id(1) == 0))
    def _():
        cnt_s[...] = jnp.zeros_like(cnt_s)

    el = lane + N_GROUPS
    oh1 = el == i1
    oh2 = el == i2
    tm = logits.shape[0]
    ti, tj = _iota2((tm, tm))
    before = (ti > tj).astype(BF16)
    pre1 = _dot(before, oh1.astype(BF16))
    pre2 = _dot(before, oh2.astype(BF16))
    tot1 = jnp.sum(oh1.astype(F32), axis=0, keepdims=True)
    tot2 = jnp.sum(oh2.astype(F32), axis=0, keepdims=True)
    base = cnt_s[...]
    rank1 = jnp.sum(jnp.where(oh1, base + pre1, 0.0), axis=-1, keepdims=True)
    rank2 = jnp.sum(jnp.where(oh2, base + tot1 + pre2, 0.0), axis=-1, keepdims=True)
    cnt_s[...] = base + tot1 + tot2
    cnt_ref[...] = cnt_s[...]
    fields = (i1 - N_GROUPS, i2 - N_GROUPS, w1, w2, rank1, rank2)
    route = jnp.zeros_like(logits)
    for n, f in enumerate(fields):
        route = jnp.where(lane == n, f, route)
    route_ref[0] = route


def _outproj(odn, ona, x, wo_dn, wo_na, gpost, gt1, gpre, sc2, sh2, wr, br, tm=512):
    bsz, s, d = x.shape
    full = lambda a: pl.BlockSpec(a.shape, lambda b, i: (0,) * a.ndim)
    mod = pl.BlockSpec((1, 1, d), lambda b, i: (b, 0, 0))
    tok = lambda n: pl.BlockSpec((1, tm, n), lambda b, i: (b, i, 0))
    return pl.pallas_call(
        _outproj_kernel,
        grid=(bsz, s // tm),
        in_specs=[tok(odn.shape[2]), tok(ona.shape[2]), tok(d), full(wo_dn), full(wo_na), full(gpost), mod,
                  full(gpre), mod, mod, full(wr), full(br)],
        out_specs=[tok(d), tok(d), tok(LANES), pl.BlockSpec((1, LANES), lambda b, i: (0, 0))],
        out_shape=[jax.ShapeDtypeStruct((bsz, s, d), F32), jax.ShapeDtypeStruct((bsz, s, d), F32),
                   jax.ShapeDtypeStruct((bsz, s, LANES), F32), jax.ShapeDtypeStruct((1, LANES), F32)],
        scratch_shapes=[pltpu.VMEM((1, LANES), F32)],
        name="outproj_route",
        compiler_params=_cparams("arbitrary", "arbitrary"),
    )(odn, ona, x, wo_dn, wo_na, gpost, gt1, gpre, sc2, sh2, wr, br)


def _dispatch_kernel(pos_ref, h_hbm, xs_in, xs_hbm, sem):
    del xs_in
    i = pl.program_id(0)
    tb = pos_ref.shape[2]

    def issue(r, carry):
        src = h_hbm.at[pl.ds(i * tb + r, 1)]
        for slot in range(2):
            pltpu.make_async_copy(src, xs_hbm.at[pl.ds(pos_ref[0, slot, r], 1)], sem).start()
        return carry

    lax.fori_loop(0, tb, issue, 0, unroll=8)
    pltpu.make_async_copy(h_hbm.at[pl.ds(0, 2 * tb)], xs_hbm.at[pl.ds(0, 2 * tb)], sem).wait()


def _dispatch(h, pos, n_rows, tb=512):
    t, d = h.shape
    return pl.pallas_call(
        _dispatch_kernel,
        grid=(t // tb,),
        in_specs=[pl.BlockSpec((1, 2, tb), lambda i: (i, 0, 0), memory_space=pltpu.SMEM),
                  pl.BlockSpec(memory_space=pl.ANY),
                  pl.BlockSpec(memory_space=pl.ANY)],
        out_specs=pl.BlockSpec(memory_space=pl.ANY),
        out_shape=jax.ShapeDtypeStruct((n_rows, d), h.dtype),
        scratch_shapes=[pltpu.SemaphoreType.DMA(())],
        input_output_aliases={2: 0},
        name="moe_dispatch",
        compiler_params=_cparams("arbitrary"),
    )(pos, h, jnp.zeros((n_rows, d), h.dtype))


def _expert_kernel(te_ref, ts_ref, nu_ref, x_ref, wg_ref, wu_ref, wd_ref, y_ref, wg_s, wu_s, wd_s):
    j = pl.program_id(0)
    valid = j < nu_ref[0]
    new_expert = (j == 0) | (te_ref[j] != te_ref[jnp.maximum(j - 1, 0)])

    @pl.when(valid & new_expert)
    def _():
        wg_s[...] = wg_ref[0].astype(BF16)
        wu_s[...] = wu_ref[0].astype(BF16)
        wd_s[...] = wd_ref[0].astype(BF16)

    @pl.when(valid)
    def _():
        x = x_ref[...].astype(BF16)
        he = (_silu(_dot(x, wg_s[...])) * _dot(x, wu_s[...])).astype(BF16)
        y_ref[...] = _dot(he, wd_s[...])

    @pl.when(jnp.logical_not(valid))
    def _():
        y_ref[...] = jnp.zeros_like(y_ref)


def _experts(xs, tile_expert, tile_src, n_used, w_gate, w_up, w_down):
    n_rows, d = xs.shape
    n_exp, _, de = w_gate.shape
    n_tiles = n_rows // MOE_ROW_TILE
    rows = pl.BlockSpec((MOE_ROW_TILE, d), lambda j, te, ts, nu: (ts[j], 0))
    grid_spec = pltpu.PrefetchScalarGridSpec(
        num_scalar_prefetch=3,
        grid=(n_tiles,),
        in_specs=[rows,
                  pl.BlockSpec((1, d, de), lambda j, te, ts, nu: (te[j], 0, 0)),
                  pl.BlockSpec((1, d, de), lambda j, te, ts, nu: (te[j], 0, 0)),
                  pl.BlockSpec((1, de, d), lambda j, te, ts, nu: (te[j], 0, 0))],
        out_specs=pl.BlockSpec((MOE_ROW_TILE, d), lambda j, te, ts, nu: (j, 0)),
        scratch_shapes=[pltpu.VMEM((d, de), BF16), pltpu.VMEM((d, de), BF16), pltpu.VMEM((de, d), BF16)],
    )
    return pl.pallas_call(
        _expert_kernel,
        grid_spec=grid_spec,
        out_shape=jax.ShapeDtypeStruct((n_rows, d), F32),
        name="moe_experts",
        compiler_params=_cparams("arbitrary"),
    )(tile_expert, tile_src, n_used, xs, w_gate, w_up, w_down)


def _combine_kernel(pos_ref, posn_ref, route_ref, x1_ref, gpost_ref, gt2_ref, ys_hbm, o_ref, buf, sems):
    i = pl.program_id(0)
    n = pl.num_programs(0)
    tc = pos_ref.shape[2]

    def start_tile(p_ref, slot):
        def issue(r, carry):
            for k in range(2):
                pltpu.make_async_copy(ys_hbm.at[pl.ds(p_ref[0, k, r], 1)], buf.at[slot, k, pl.ds(r, 1)],
                                      sems.at[slot]).start()
            return carry
        lax.fori_loop(0, tc, issue, 0, unroll=8)

    @pl.when(i == 0)
    def _():
        start_tile(pos_ref, 0)

    @pl.when(i + 1 < n)
    def _():
        start_tile(posn_ref, (i + 1) % 2)

    slot = i % 2
    pltpu.make_async_copy(buf.at[slot], buf.at[slot], sems.at[slot]).wait()
    route = route_ref[...]
    lane = lax.broadcasted_iota(jnp.int32, route.shape, 1)
    w1 = jnp.sum(jnp.where(lane == 2, route, 0.0), axis=-1, keepdims=True)
    w2 = jnp.sum(jnp.where(lane == 3, route, 0.0), axis=-1, keepdims=True)
    y = w1 * buf[slot, 0] + w2 * buf[slot, 1]
    o_ref[...] = x1_ref[...] + gt2_ref[0] * (_rms(y) * gpost_ref[...])


def _combine(ys, pos, route, x1, gpost, gt2, seq, tc=256):
    t, d = x1.shape
    n = t // tc
    per_seq = seq // tc
    smem = lambda f: pl.BlockSpec((1, 2, tc), f, memory_space=pltpu.SMEM)
    return pl.pallas_call(
        _combine_kernel,
        grid=(n,),
        in_specs=[smem(lambda i: (i, 0, 0)), smem(lambda i: (jnp.minimum(i + 1, n - 1), 0, 0)),
                  pl.BlockSpec((tc, LANES), lambda i: (i, 0)),
                  pl.BlockSpec((tc, d), lambda i: (i, 0)),
                  pl.BlockSpec((1, d), lambda i: (0, 0)),
                  pl.BlockSpec((1, 1, d), lambda i: (i // per_seq, 0, 0)),
                  pl.BlockSpec(memory_space=pl.ANY)],
        out_specs=pl.BlockSpec((tc, d), lambda i: (i, 0)),
        out_shape=jax.ShapeDtypeStruct((t, d), F32),
        scratch_shapes=[pltpu.VMEM((2, 2, tc, d), F32), pltpu.SemaphoreType.DMA((2,))],
        name="moe_combine",
        compiler_params=_cparams("arbitrary"),
    )(pos, pos, route, x1, gpost, gt2, ys)


def _moe(h, route, counts, w_gate, w_up, w_down, x1, gpost, gt2):
    bsz, s, d = x1.shape
    t = bsz * s
    n_exp = w_gate.shape[0]
    tr = MOE_ROW_TILE
    n_tiles = 2 * t // tr + n_exp
    route = route.reshape(t, LANES)
    ids = route[:, 0:2].astype(jnp.int32)
    ranks = route[:, 4:6].astype(jnp.int32)
    cnt = counts[0, :n_exp].astype(jnp.int32)
    padded = (cnt + tr - 1) // tr * tr
    ends = jnp.cumsum(padded)
    offs = ends - padded
    onehot = ids[:, :, None] == jnp.arange(n_exp, dtype=jnp.int32)[None, None, :]
    pos = jnp.sum(jnp.where(onehot, offs[None, None, :], 0), axis=-1) + ranks
    n_used = ends[-1] // tr
    tile = jnp.arange(n_tiles, dtype=jnp.int32)
    tile_expert = jnp.sum((tile[:, None] >= (ends // tr)[None, :]).astype(jnp.int32), axis=-1)
    last = n_used - 1
    valid = tile < n_used
    tile_src = jnp.where(valid, tile, last)
    tile_expert = jnp.minimum(jnp.where(valid, tile_expert, jnp.sum((last >= ends // tr).astype(jnp.int32))),
                              n_exp - 1)

    def tiles_of(tok_tile):
        return pos.reshape(t // tok_tile, tok_tile, 2).transpose(0, 2, 1)

    xs = _dispatch(h.reshape(t, d), tiles_of(512), n_tiles * tr)
    ys = _experts(xs, tile_expert, tile_src, n_used.reshape(1), w_gate, w_up, w_down)
    out = _combine(ys, tiles_of(256), route, x1.reshape(t, d), gpost, gt2, s)
    return out.reshape(bsz, s, d)


def _pad_lanes(a, n=LANES):
    return jnp.pad(a, [(0, 0)] * (a.ndim - 1) + [(0, n - a.shape[-1])])


def kernel(x, c, w_ada, b_ada, g_pre_mix, g_post_mix, w_in, w_conv_dn, a_log_dn, dt_bias_dn, g_onorm_dn, rpb_na,
           w_out, g_pre_ffn, g_post_ffn, w_group, b_group, w_expert, b_expert, w_gate, w_up, w_down):
    bsz, s, d = x.shape
    depth = w_ada.shape[0]
    n_dn = a_log_dn.shape[2]
    dn_w = n_dn * HEAD_DIM
    for l in range(depth):
        ada = _ada(c, w_ada[l], b_ada[l])
        sh1, sc1, gt1, sh2, sc2, gt2 = [a.reshape(bsz, 1, d) for a in jnp.split(ada, 6, axis=-1)]

        wi = w_in[l]
        o_z, o_ab, o_na = 3 * dn_w, 4 * dn_w, 4 * dn_w + 4 * n_dn
        wdn = wi[:, :o_z].astype(BF16)
        wz = wi[:, o_z:o_ab].astype(BF16)
        wab = _pad_lanes(wi[:, o_ab:o_na]).astype(BF16)
        na_w = (wi.shape[1] - o_na) // 3
        wna = jnp.concatenate([wi[:, o_na:o_na + na_w] * (HEAD_DIM ** -0.5), wi[:, o_na + na_w:]], axis=1).astype(BF16)
        alog = _pad_lanes(a_log_dn[l].reshape(1, 2 * n_dn))
        dtb = _pad_lanes(dt_bias_dn[l].reshape(1, 2 * n_dn))
        dn, z, gates, na = _inproj(x, sc1, sh1, g_pre_mix[l].reshape(1, d), wdn, wz, wab, wna, alog, dtb)

        gt = gates[:, :, :32].reshape(bsz, s // DN_CHUNK, DN_CHUNK, 32).transpose(0, 1, 3, 2)
        gon2 = jnp.tile(g_onorm_dn[l].reshape(1, HEAD_DIM), (1, LANES // HEAD_DIM))
        o_dn = _delta(dn, w_conv_dn[l], gates, gt, z, gon2)

        o_na = _natten(na, _na_bias_tables(rpb_na[l]))

        wo = w_out[l].astype(BF16)
        wr = _pad_lanes(jnp.concatenate([w_group[l], w_expert[l]], axis=1))
        br = _pad_lanes(jnp.concatenate([b_group[l], b_expert[l]]).reshape(1, -1))
        x1, h2, route, counts = _outproj(o_dn, o_na, x, wo[:dn_w], wo[dn_w:], g_post_mix[l].reshape(1, d), gt1,
                                g_pre_ffn[l].reshape(1, d), sc2, sh2, wr, br)
        x = _moe(h2, route, counts, w_gate[l], w_up[l], w_down[l], x1, g_post_ffn[l].reshape(1, d), gt2)
    return x
```

```python
import numpy as np
import jax
import jax.numpy as jnp
from jax import lax
from jax.experimental import pallas as pl
from jax.experimental.pallas import tpu as pltpu

F32 = jnp.float32
BF16 = jnp.bfloat16
HIGHEST = lax.Precision.HIGHEST

GRID_W = 64
HEAD_DIM = 64
DN_CONV = 5
DN_CHUNK = 128
NA_WIN_H = 8
NA_WIN_W = 16
N_GROUPS = 4
EXPERTS_PER_GROUP = 8
N_EXPERTS = N_GROUPS * EXPERTS_PER_GROUP
EPS = 1e-6
LANES = 128
NEG_BIG = -1e30
VMEM_LIMIT = 56 * 1024 * 1024
MOE_ROW_TILE = 256


def _cparams(*sem):
    return pltpu.CompilerParams(dimension_semantics=sem, vmem_limit_bytes=VMEM_LIMIT)


def _sigmoid(x):
    return 1.0 / (1.0 + jnp.exp(-x))


def _silu(x):
    return x * _sigmoid(x)


def _softplus(x):
    return jnp.maximum(x, 0.0) + jnp.log1p(jnp.exp(-jnp.abs(x)))


def _dot(a, b, **kw):
    return jnp.dot(a, b, preferred_element_type=F32, **kw)


def _dot_nt(a, b, **kw):
    return lax.dot_general(a, b, (((1,), (1,)), ((), ())), preferred_element_type=F32, **kw)


def _dot_tn(a, b, **kw):
    return lax.dot_general(a, b, (((0,), (0,)), ((), ())), preferred_element_type=F32, **kw)


def _rms(x):
    return x * lax.rsqrt(jnp.mean(x * x, axis=-1, keepdims=True) + EPS)


def _iota2(shape):
    return lax.broadcasted_iota(jnp.int32, shape, 0), lax.broadcasted_iota(jnp.int32, shape, 1)


def _ada_kernel(c_ref, w_ref, b_ref, o_ref):
    cond = _silu(c_ref[...])
    o_ref[...] = _dot(cond, w_ref[...], precision=HIGHEST) + b_ref[...]


def _ada(c, w_ada, b_ada):
    bsz, d = c.shape
    n = w_ada.shape[1]
    tn = 1536
    return pl.pallas_call(
        _ada_kernel,
        grid=(n // tn,),
        in_specs=[pl.BlockSpec((bsz, d), lambda j: (0, 0)),
                  pl.BlockSpec((d, tn), lambda j: (0, j)),
                  pl.BlockSpec((1, tn), lambda j: (0, j))],
        out_specs=pl.BlockSpec((bsz, tn), lambda j: (0, j)),
        out_shape=jax.ShapeDtypeStruct((bsz, n), F32),
        name="ada_ln",
        compiler_params=_cparams("arbitrary"),
    )(c, w_ada, b_ada.reshape(1, n))


def _inproj_kernel(x_ref, sc_ref, sh_ref, g_ref, wdn_ref, wz_ref, wab_ref, wna_ref, alog_ref, dtb_ref,
                   dn_ref, z_ref, gate_ref, na_ref):
    h = _rms(x_ref[0]) * g_ref[...]
    h = h * (1.0 + sc_ref[0]) + sh_ref[0]
    hb = h.astype(BF16)
    dn_ref[0] = _dot(hb, wdn_ref[...]).astype(BF16)
    z_ref[0] = _dot(hb, wz_ref[...]).astype(BF16)
    na_ref[0] = _dot(hb, wna_ref[...]).astype(BF16)
    ab = _dot(hb, wab_ref[...])
    lane = lax.broadcasted_iota(jnp.int32, ab.shape, 1)
    decay = -jnp.exp(alog_ref[...]) * _softplus(ab + dtb_ref[...])
    gate_ref[0] = jnp.where(lane < 16, decay, _sigmoid(ab))


def _inproj(x, sc, sh, g, wdn, wz, wab, wna, alog, dtb, tm=512):
    bsz, s, d = x.shape
    ndn, nz, nna = wdn.shape[1], wz.shape[1], wna.shape[1]
    full = lambda a: pl.BlockSpec(a.shape, lambda b, i: (0,) * a.ndim)
    mod = pl.BlockSpec((1, 1, d), lambda b, i: (b, 0, 0))
    tok = lambda n: pl.BlockSpec((1, tm, n), lambda b, i: (b, i, 0))
    return pl.pallas_call(
        _inproj_kernel,
        grid=(bsz, s // tm),
        in_specs=[tok(d), mod, mod, full(g), full(wdn), full(wz), full(wab), full(wna), full(alog), full(dtb)],
        out_specs=[tok(ndn), tok(nz), tok(LANES), tok(nna)],
        out_shape=[jax.ShapeDtypeStruct((bsz, s, ndn), BF16), jax.ShapeDtypeStruct((bsz, s, nz), BF16),
                   jax.ShapeDtypeStruct((bsz, s, LANES), F32), jax.ShapeDtypeStruct((bsz, s, nna), BF16)],
        name="prenorm_inproj",
        compiler_params=_cparams("arbitrary", "arbitrary"),
    )(x, sc, sh, g, wdn, wz, wab, wna, alog, dtb)


def _conv_silu(x, w):
    s = x.shape[0]
    row = lax.broadcasted_iota(jnp.int32, x.shape, 0)
    pad = (DN_CONV - 1) // 2
    acc = x * w[pad:pad + 1]
    for j in range(DN_CONV):
        d = j - pad
        if d == 0:
            continue
        xs = pltpu.roll(x, (-d) % s, 0)
        ok = (row + d >= 0) & (row + d < s)
        acc = acc + jnp.where(ok, xs, 0.0) * w[j:j + 1]
    return _silu(acc)


def _chunk_cumsum(g, reverse):
    s = g.shape[0]
    pos = lax.broadcasted_iota(jnp.int32, g.shape, 0) & (DN_CHUNK - 1)
    sh = 1
    while sh < DN_CHUNK:
        if reverse:
            g = g + jnp.where(pos + sh < DN_CHUNK, pltpu.roll(g, s - sh, 0), 0.0)
        else:
            g = g + jnp.where(pos >= sh, pltpu.roll(g, sh, 0), 0.0)
        sh *= 2
    return g


def _delta_kernel(q_ref, k_ref, v_ref, wq_ref, wk_ref, wv_ref, gate_ref, gt_ref, z_ref, gon_ref, o_ref,
                  gcr_s, k_s, q_s, kb_s, kbg_s, qd_s, kt_s, vb_s, gci_s, egl_s, m_s, t0_s, t1_s, at_s, oacc_s):
    hp = pl.program_id(1)
    s = q_ref.shape[1]
    c = DN_CHUNK
    hd = HEAD_DIM
    n_chunks = s // c

    li, lj = _iota2((LANES, LANES))
    same_head_b = lax.shift_right_logical(li, 6) == lax.shift_right_logical(lj, 6)
    same_head = same_head_b.astype(F32)

    def l2n(x):
        return x * lax.rsqrt(_dot(x * x, same_head) + EPS)

    q2 = l2n(_conv_silu(q_ref[0].astype(F32), wq_ref[...])) * (hd ** -0.5)
    k2 = l2n(_conv_silu(k_ref[0].astype(F32), wk_ref[...]))
    v2 = _conv_silu(v_ref[0].astype(F32), wv_ref[...])
    k_s[...] = k2.astype(BF16)
    q_s[...] = q2.astype(BF16)

    gates = gate_ref[0]
    lane_s = lax.broadcasted_iota(jnp.int32, gates.shape, 1)

    def pair_cols(base):
        c0 = base + 2 * hp
        col0 = jnp.sum(jnp.where(lane_s == c0, gates, 0.0), axis=-1, keepdims=True)
        col1 = jnp.sum(jnp.where(lane_s == c0 + 1, gates, 0.0), axis=-1, keepdims=True)
        return jnp.where(lane_s < hd, col0, col1)

    ri, rj = _iota2((c, c))
    eye = (ri == rj).astype(F32)
    gt = gt_ref[0].reshape(n_chunks * 32, c)

    for direction in range(2):
        g2 = pair_cols(8 * direction)
        beta2 = pair_cols(16 + 8 * direction)
        gci2 = _chunk_cumsum(g2, reverse=direction == 1)
        gc3 = gci2.reshape(n_chunks, c, LANES)
        tot = gc3[:, c - 1:c, :] if direction == 0 else gc3[:, 0:1, :]
        gl2 = jnp.broadcast_to(tot, gc3.shape).reshape(s, LANES)
        egc2 = jnp.exp(gci2)
        kb2 = k2 * beta2
        kb_s[direction] = kb2.astype(BF16)
        kbg_s[direction] = (kb2 * egc2).astype(BF16)
        qd_s[direction] = (q2 * egc2).astype(BF16)
        kt_s[direction] = (k2 * jnp.exp(gl2 - gci2)).astype(BF16)
        vb_s[direction] = v2 * beta2
        gci_s[direction] = gci2
        egl_s[direction] = jnp.exp(gl2)
        cum_row = (ri <= rj) if direction == 0 else (ri >= rj)
        gcr_s[direction] = _dot(gt, cum_row.astype(F32), precision=HIGHEST).reshape(n_chunks, 32, c)

    lane_c = lax.broadcasted_iota(jnp.int32, (c, LANES), 1)
    head0 = lane_c < hd

    same = lambda sh: lax.shift_right_logical(ri, sh) == lax.shift_right_logical(rj, sh)
    group = 8
    n_levels = c.bit_length() - 2
    t_bufs = (t0_s, t1_s)

    def chunk_rows(g, j):
        return pl.ds(pl.multiple_of((g * group + j) * c, c), c)

    def chunk_matrices(g, carry):
        for j in range(group):
            rows = chunk_rows(g, j)
            kc = k_s[rows, :]
            qc = q_s[rows, :]
            zero = jnp.zeros_like(kc)
            for direction in range(2):
                incl = (ri >= rj) if direction == 0 else (ri <= rj)
                strict = (ri > rj) if direction == 0 else (ri < rj)
                kbc = kb_s[direction, rows, :]
                gci = gci_s[direction, rows, :]
                for hh in range(2):
                    hm = head0 if hh == 0 else jnp.logical_not(head0)
                    gi = jnp.broadcast_to(gci[:, hh * hd:hh * hd + 1], (c, c))
                    gj = jnp.broadcast_to(
                        gcr_s[direction, g * group + j, pl.ds(8 * direction + 2 * hp + hh, 1), :], (c, c))
                    decay = jnp.where(incl, jnp.exp(jnp.where(incl, gi - gj, 0.0)), 0.0)
                    m = jnp.where(strict, _dot_nt(jnp.where(hm, kbc, zero), kc) * decay, 0.0)
                    m_s[direction, hh, rows, :] = m.astype(BF16)
                    t_bufs[n_levels % 2][direction, hh, rows, :] =(eye - jnp.where(same(1), m, 0.0)).astype(BF16)
                    at_s[direction, hh, rows, :] = (_dot_nt(jnp.where(hm, qc, zero), kc) * decay).astype(BF16)
        return carry

    lax.fori_loop(0, n_chunks // group, chunk_matrices, 0)

    for level in range(n_levels):
        def merge_level(g, carry, level=level):
            sh = level + 1
            src = (n_levels - level) % 2
            cmask = same(sh + 1) & jnp.logical_not(same(sh))
            for j in range(group):
                rows = chunk_rows(g, j)
                for direction in range(2):
                    for hh in range(2):
                        tb = t_bufs[src][direction, hh, rows, :]
                        mb = m_s[direction, hh, rows, :]
                        cm = jnp.where(cmask, mb, jnp.zeros_like(mb))
                        y = _dot(tb, _dot(cm, tb).astype(BF16))
                        t_bufs[1 - src][direction, hh, rows, :] =jnp.where(cmask, (-y).astype(BF16), tb)
            return carry

        lax.fori_loop(0, n_chunks // group, merge_level, 0)

    oacc_s[...] = jnp.zeros_like(oacc_s)

    def scan_step(i, states):
        new_states = []
        for direction in range(2):
            n = i if direction == 0 else n_chunks - 1 - i
            r0 = pl.multiple_of(n * c, c)
            rows = pl.ds(r0, c)
            sb = states[direction].astype(BF16)
            rhs = (vb_s[direction, rows, :] - _dot(kbg_s[direction, rows, :], sb)).astype(BF16)
            v_new = jnp.where(head0, _dot(t0_s[direction, 0, rows, :], rhs),
                              _dot(t0_s[direction, 1, rows, :], rhs)).astype(BF16)
            intra = jnp.where(head0, _dot(at_s[direction, 0, rows, :], v_new),
                              _dot(at_s[direction, 1, rows, :], v_new))
            oacc_s[rows, :] += _dot(qd_s[direction, rows, :], sb) + intra
            upd = _dot_tn(kt_s[direction, rows, :], v_new)
            new_states.append(states[direction] * egl_s[direction, pl.ds(r0, 1), :]
                              + jnp.where(same_head_b, upd, 0.0))
        return tuple(new_states)

    zero_state = jnp.zeros((LANES, LANES), F32)
    lax.fori_loop(0, n_chunks, scan_step, (zero_state, zero_state))

    o2 = oacc_s[...]
    ms = _dot(o2 * o2, same_head) * (1.0 / hd)
    o2 = o2 * lax.rsqrt(ms + EPS) * gon_ref[...]
    o_ref[0] = (o2 * _silu(z_ref[0].astype(F32))).astype(BF16)


def _delta(dn, w_conv, gates, gt, z, gon2):
    bsz, s, _ = dn.shape
    n_pairs = z.shape[2] // LANES
    c = DN_CHUNK
    col = lambda off: pl.BlockSpec((1, s, LANES), lambda b, p: (b, 0, off + p))
    wcol = lambda off: pl.BlockSpec((DN_CONV, LANES), lambda b, p: (0, off + p))
    two = lambda dt: pltpu.VMEM((2, s, LANES), dt)
    return pl.pallas_call(
        _delta_kernel,
        grid=(bsz, n_pairs),
        in_specs=[col(0), col(n_pairs), col(2 * n_pairs), wcol(0), wcol(n_pairs), wcol(2 * n_pairs),
                  pl.BlockSpec((1, s, LANES), lambda b, p: (b, 0, 0)),
                  pl.BlockSpec((1, s // c, 32, c), lambda b, p: (b, 0, 0, 0)),
                  col(0),
                  pl.BlockSpec((1, LANES), lambda b, p: (0, 0))],
        out_specs=col(0),
        out_shape=jax.ShapeDtypeStruct((bsz, s, n_pairs * LANES), BF16),
        scratch_shapes=[pltpu.VMEM((2, s // c, 32, c), F32),
                        pltpu.VMEM((s, LANES), BF16), pltpu.VMEM((s, LANES), BF16),
                        two(BF16), two(BF16), two(BF16), two(BF16),
                        two(F32), two(F32), two(F32),
                        pltpu.VMEM((2, 2, s, LANES), BF16), pltpu.VMEM((2, 2, s, LANES), BF16),
                        pltpu.VMEM((2, 2, s, LANES), BF16), pltpu.VMEM((2, 2, s, LANES), BF16),
                        pltpu.VMEM((s, LANES), F32)],
        name="delta_rule",
        compiler_params=_cparams("arbitrary", "arbitrary"),
    )(dn, dn, dn, w_conv, w_conv, w_conv, gates, gt, z, gon2)


def _na_bias_tables(rpb):
    n_heads = rpb.shape[0]
    cc = np.arange(GRID_W)
    c0 = np.clip(cc - NA_WIN_W // 2, 0, GRID_W - NA_WIN_W)
    kc = np.arange(GRID_W)
    in_win = (kc[None, :] >= c0[:, None]) & (kc[None, :] < c0[:, None] + NA_WIN_W)
    col_off = kc[None, :] - cc[:, None] + NA_WIN_W - 1
    onehot = (col_off[None] == np.arange(2 * NA_WIN_W - 1)[:, None, None]) & in_win[None]
    rows = jnp.stack([rpb[:, NA_WIN_H - 1 - d:2 * NA_WIN_H - 1 - d, :] for d in range(NA_WIN_H)], axis=1)
    rows = rows.reshape(n_heads // 2, 2, NA_WIN_H, NA_WIN_H, 2 * NA_WIN_W - 1).astype(F32)
    bias = jnp.einsum("phdic,cqk->pdhqik", rows, jnp.asarray(onehot, F32), precision=HIGHEST)
    bias = bias + jnp.asarray(np.where(in_win, 0.0, NEG_BIG), F32)[None, None, None, :, None, :]
    return bias.reshape(n_heads // 2, NA_WIN_H, 2 * GRID_W, NA_WIN_H * GRID_W)


def _natten_kernel(q_ref, k_ref, v_ref, bias_ref, o_ref):
    s = q_ref.shape[1]
    rows = s // GRID_W
    w = GRID_W
    win = min(NA_WIN_H, rows)
    hd = HEAD_DIM
    lane = lax.broadcasted_iota(jnp.int32, (w, LANES), 1)
    first = lane < hd
    for r in range(rows):
        r0 = min(max(r - win // 2, 0), rows - win)
        d = r - r0
        q = q_ref[0, r * w:(r + 1) * w, :]
        zero = jnp.zeros_like(q)
        q_st = jnp.concatenate([jnp.where(first, q, zero), jnp.where(first, zero, q)], axis=0)
        kw = k_ref[0, r0 * w:(r0 + win) * w, :]
        vw = v_ref[0, r0 * w:(r0 + win) * w, :]
        sc = _dot_nt(q_st, kw) + bias_ref[0, d]
        sc = sc - jnp.max(sc, axis=-1, keepdims=True)
        p = jnp.exp(sc)
        denom = jnp.sum(p, axis=-1, keepdims=True)
        o = _dot(p.astype(BF16), vw) / denom
        o_ref[0, r * w:(r + 1) * w, :] = jnp.where(first, o[:w], o[w:]).astype(BF16)


def _natten(na, bias):
    bsz, s, n3 = na.shape
    n_pairs = n3 // (3 * LANES)
    col = lambda off: pl.BlockSpec((1, s, LANES), lambda p, b: (b, 0, off + p))
    return pl.pallas_call(
        _natten_kernel,
        grid=(n_pairs, bsz),
        in_specs=[col(0), col(n_pairs), col(2 * n_pairs),
                  pl.BlockSpec((1,) + bias.shape[1:], lambda p, b: (p, 0, 0, 0))],
        out_specs=col(0),
        out_shape=jax.ShapeDtypeStruct((bsz, s, n_pairs * LANES), BF16),
        name="natten",
        compiler_params=_cparams("arbitrary", "arbitrary"),
    )(na, na, na, bias)


def _outproj_kernel(odn_ref, ona_ref, x_ref, wo_dn_ref, wo_na_ref, gpost_ref, gt1_ref, gpre_ref, sc_ref, sh_ref,
                    wr_ref, br_ref, x1_ref, h_ref, route_ref, cnt_ref, cnt_s):
    y = _dot(odn_ref[0], wo_dn_ref[...]) + _dot(ona_ref[0], wo_na_ref[...])
    x1 = x_ref[0] + gt1_ref[0] * (_rms(y) * gpost_ref[...])
    x1_ref[0] = x1
    h = _rms(x1) * gpre_ref[...]
    h = h * (1.0 + sc_ref[0]) + sh_ref[0]
    h_ref[0] = h

    logits = _dot(h, wr_ref[...], precision=HIGHEST) + br_ref[...]
    lane = lax.broadcasted_iota(jnp.int32, logits.shape, 1).astype(F32)
    big = float(LANES)

    def masked_softmax(mask):
        lm = jnp.where(mask, logits, NEG_BIG)
        e = jnp.where(mask, jnp.exp(lm - jnp.max(lm, axis=-1, keepdims=True)), 0.0)
        return e / jnp.sum(e, axis=-1, keepdims=True)

    def top1(p, mask):
        pm = jnp.where(mask, p, -1.0)
        best = jnp.max(pm, axis=-1, keepdims=True)
        idx = jnp.min(jnp.where(mask & (pm == best), lane, big), axis=-1, keepdims=True)
        return best, idx

    gmask = lane < N_GROUPS
    gp_top, g_idx = top1(masked_softmax(gmask), gmask)
    e_lo = N_GROUPS + g_idx * EXPERTS_PER_GROUP
    emask = (lane >= e_lo) & (lane < e_lo + EXPERTS_PER_GROUP)
    pe = masked_softmax(emask)
    p1, i1 = top1(pe, emask)
    p2, i2 = top1(pe, emask & (lane != i1))
    w1 = gp_top * p1 / (p1 + p2)
    w2 = gp_top * p2 / (p1 + p2)

    @pl.when((pl.program_id(0) == 0) & (pl.program_id(1) == 0))
    def _():
        cnt_s[...] = jnp.zeros_like(cnt_s)

    el = lane + N_GROUPS
    oh1 = el == i1
    oh2 = el == i2
    tm = logits.shape[0]
    ti, tj = _iota2((tm, tm))
    before = (ti > tj).astype(BF16)
    pre1 = _dot(before, oh1.astype(BF16))
    pre2 = _dot(before, oh2.astype(BF16))
    tot1 = jnp.sum(oh1.astype(F32), axis=0, keepdims=True)
    tot2 = jnp.sum(oh2.astype(F32), axis=0, keepdims=True)
    base = cnt_s[...]
    rank1 = jnp.sum(jnp.where(oh1, base + pre1, 0.0), axis=-1, keepdims=True)
    rank2 = jnp.sum(jnp.where(oh2, base + tot1 + pre2, 0.0), axis=-1, keepdims=True)
    cnt_s[...] = base + tot1 + tot2
    cnt_ref[...] = cnt_s[...]
    fields = (i1 - N_GROUPS, i2 - N_GROUPS, w1, w2, rank1, rank2)
    route = jnp.zeros_like(logits)
    for n, f in enumerate(fields):
        route = jnp.where(lane == n, f, route)
    route_ref[0] = route


def _outproj(odn, ona, x, wo_dn, wo_na, gpost, gt1, gpre, sc2, sh2, wr, br, tm=512):
    bsz, s, d = x.shape
    full = lambda a: pl.BlockSpec(a.shape, lambda b, i: (0,) * a.ndim)
    mod = pl.BlockSpec((1, 1, d), lambda b, i: (b, 0, 0))
    tok = lambda n: pl.BlockSpec((1, tm, n), lambda b, i: (b, i, 0))
    return pl.pallas_call(
        _outproj_kernel,
        grid=(bsz, s // tm),
        in_specs=[tok(odn.shape[2]), tok(ona.shape[2]), tok(d), full(wo_dn), full(wo_na), full(gpost), mod,
                  full(gpre), mod, mod, full(wr), full(br)],
        out_specs=[tok(d), tok(d), tok(LANES), pl.BlockSpec((1, LANES), lambda b, i: (0, 0))],
        out_shape=[jax.ShapeDtypeStruct((bsz, s, d), F32), jax.ShapeDtypeStruct((bsz, s, d), F32),
                   jax.ShapeDtypeStruct((bsz, s, LANES), F32), jax.ShapeDtypeStruct((1, LANES), F32)],
        scratch_shapes=[pltpu.VMEM((1, LANES), F32)],
        name="outproj_route",
        compiler_params=_cparams("arbitrary", "arbitrary"),
    )(odn, ona, x, wo_dn, wo_na, gpost, gt1, gpre, sc2, sh2, wr, br)


def _dispatch_kernel(pos_ref, h_ref, xs_in, xs_hbm, sem):
    del xs_in
    tb = pos_ref.shape[2]

    def issue(r, carry):
        src = h_ref.at[pl.ds(r, 1)]
        for slot in range(2):
            pltpu.make_async_copy(src, xs_hbm.at[pl.ds(pos_ref[0, slot, r], 1)], sem).start()
        return carry

    lax.fori_loop(0, tb, issue, 0, unroll=8)
    for _ in range(2):
        pltpu.make_async_copy(h_ref, xs_hbm.at[pl.ds(0, tb)], sem).wait()


def _dispatch(h, pos, n_rows, tb=512):
    t, d = h.shape
    return pl.pallas_call(
        _dispatch_kernel,
        grid=(t // tb,),
        in_specs=[pl.BlockSpec((1, 2, tb), lambda i: (i, 0, 0), memory_space=pltpu.SMEM),
                  pl.BlockSpec((tb, d), lambda i: (i, 0)),
                  pl.BlockSpec(memory_space=pl.ANY)],
        out_specs=pl.BlockSpec(memory_space=pl.ANY),
        out_shape=jax.ShapeDtypeStruct((n_rows, d), h.dtype),
        scratch_shapes=[pltpu.SemaphoreType.DMA(())],
        input_output_aliases={2: 0},
        name="moe_dispatch",
        compiler_params=_cparams("arbitrary"),
    )(pos, h, jnp.zeros((n_rows, d), h.dtype))


def _expert_kernel(te_ref, ts_ref, nu_ref, x_ref, wg_ref, wu_ref, wd_ref, y_ref, wg_s, wu_s, wd_s):
    j = pl.program_id(0)
    valid = j < nu_ref[0]
    new_expert = (j == 0) | (te_ref[j] != te_ref[jnp.maximum(j - 1, 0)])

    @pl.when(valid & new_expert)
    def _():
        wg_s[...] = wg_ref[0].astype(BF16)
        wu_s[...] = wu_ref[0].astype(BF16)
        wd_s[...] = wd_ref[0].astype(BF16)

    @pl.when(valid)
    def _():
        x = x_ref[...].astype(BF16)
        he = (_silu(_dot(x, wg_s[...])) * _dot(x, wu_s[...])).astype(BF16)
        y_ref[...] = _dot(he, wd_s[...])

    @pl.when(jnp.logical_not(valid))
    def _():
        y_ref[...] = jnp.zeros_like(y_ref)


def _experts(xs, tile_expert, tile_src, n_used, w_gate, w_up, w_down):
    n_rows, d = xs.shape
    n_exp, _, de = w_gate.shape
    n_tiles = n_rows // MOE_ROW_TILE
    rows = pl.BlockSpec((MOE_ROW_TILE, d), lambda j, te, ts, nu: (ts[j], 0))
    grid_spec = pltpu.PrefetchScalarGridSpec(
        num_scalar_prefetch=3,
        grid=(n_tiles,),
        in_specs=[rows,
                  pl.BlockSpec((1, d, de), lambda j, te, ts, nu: (te[j], 0, 0)),
                  pl.BlockSpec((1, d, de), lambda j, te, ts, nu: (te[j], 0, 0)),
                  pl.BlockSpec((1, de, d), lambda j, te, ts, nu: (te[j], 0, 0))],
        out_specs=pl.BlockSpec((MOE_ROW_TILE, d), lambda j, te, ts, nu: (j, 0)),
        scratch_shapes=[pltpu.VMEM((d, de), BF16), pltpu.VMEM((d, de), BF16), pltpu.VMEM((de, d), BF16)],
    )
    return pl.pallas_call(
        _expert_kernel,
        grid_spec=grid_spec,
        out_shape=jax.ShapeDtypeStruct((n_rows, d), F32),
        name="moe_experts",
        compiler_params=_cparams("arbitrary"),
    )(tile_expert, tile_src, n_used, xs, w_gate, w_up, w_down)


def _combine_kernel(pos_ref, posn_ref, route_ref, x1_ref, gpost_ref, gt2_ref, ys_hbm, o_ref, buf, sems):
    i = pl.program_id(0)
    n = pl.num_programs(0)
    tc = pos_ref.shape[2]

    def start_tile(p_ref, slot):
        def issue(r, carry):
            for k in range(2):
                pltpu.make_async_copy(ys_hbm.at[pl.ds(p_ref[0, k, r], 1)], buf.at[slot, k, pl.ds(r, 1)],
                                      sems.at[slot]).start()
            return carry
        lax.fori_loop(0, tc, issue, 0, unroll=8)

    @pl.when(i == 0)
    def _():
        start_tile(pos_ref, 0)

    @pl.when(i + 1 < n)
    def _():
        start_tile(posn_ref, (i + 1) % 2)

    slot = i % 2
    pltpu.make_async_copy(buf.at[slot], buf.at[slot], sems.at[slot]).wait()
    route = route_ref[...]
    lane = lax.broadcasted_iota(jnp.int32, route.shape, 1)
    w1 = jnp.sum(jnp.where(lane == 2, route, 0.0), axis=-1, keepdims=True)
    w2 = jnp.sum(jnp.where(lane == 3, route, 0.0), axis=-1, keepdims=True)
    y = w1 * buf[slot, 0] + w2 * buf[slot, 1]
    o_ref[...] = x1_ref[...] + gt2_ref[0] * (_rms(y) * gpost_ref[...])


def _combine(ys, pos, route, x1, gpost, gt2, seq, tc=256):
    t, d = x1.shape
    n = t // tc
    per_seq = seq // tc
    smem = lambda f: pl.BlockSpec((1, 2, tc), f, memory_space=pltpu.SMEM)
    return pl.pallas_call(
        _combine_kernel,
        grid=(n,),
        in_specs=[smem(lambda i: (i, 0, 0)), smem(lambda i: (jnp.minimum(i + 1, n - 1), 0, 0)),
                  pl.BlockSpec((tc, LANES), lambda i: (i, 0)),
                  pl.BlockSpec((tc, d), lambda i: (i, 0)),
                  pl.BlockSpec((1, d), lambda i: (0, 0)),
                  pl.BlockSpec((1, 1, d), lambda i: (i // per_seq, 0, 0)),
                  pl.BlockSpec(memory_space=pl.ANY)],
        out_specs=pl.BlockSpec((tc, d), lambda i: (i, 0)),
        out_shape=jax.ShapeDtypeStruct((t, d), F32),
        scratch_shapes=[pltpu.VMEM((2, 2, tc, d), F32), pltpu.SemaphoreType.DMA((2,))],
        name="moe_combine",
        compiler_params=_cparams("arbitrary"),
    )(pos, pos, route, x1, gpost, gt2, ys)


def _moe(h, route, counts, w_gate, w_up, w_down, x1, gpost, gt2):
    bsz, s, d = x1.shape
    t = bsz * s
    n_exp = w_gate.shape[0]
    tr = MOE_ROW_TILE
    n_tiles = 2 * t // tr + n_exp
    route = route.reshape(t, LANES)
    ids = route[:, 0:2].astype(jnp.int32)
    ranks = route[:, 4:6].astype(jnp.int32)
    cnt = counts[0, :n_exp].astype(jnp.int32)
    padded = (cnt + tr - 1) // tr * tr
    ends = jnp.cumsum(padded)
    offs = ends - padded
    onehot = ids[:, :, None] == jnp.arange(n_exp, dtype=jnp.int32)[None, None, :]
    pos = jnp.sum(jnp.where(onehot, offs[None, None, :], 0), axis=-1) + ranks
    n_used = ends[-1] // tr
    tile = jnp.arange(n_tiles, dtype=jnp.int32)
    tile_expert = jnp.sum((tile[:, None] >= (ends // tr)[None, :]).astype(jnp.int32), axis=-1)
    last = n_used - 1
    valid = tile < n_used
    tile_src = jnp.where(valid, tile, last)
    tile_expert = jnp.minimum(jnp.where(valid, tile_expert, jnp.sum((last >= ends // tr).astype(jnp.int32))),
                              n_exp - 1)

    def tiles_of(tok_tile):
        return pos.reshape(t // tok_tile, tok_tile, 2).transpose(0, 2, 1)

    xs = _dispatch(h.reshape(t, d), tiles_of(512), n_tiles * tr)
    ys = _experts(xs, tile_expert, tile_src, n_used.reshape(1), w_gate, w_up, w_down)
    out = _combine(ys, tiles_of(256), route, x1.reshape(t, d), gpost, gt2, s)
    return out.reshape(bsz, s, d)


def _pad_lanes(a, n=LANES):
    return jnp.pad(a, [(0, 0)] * (a.ndim - 1) + [(0, n - a.shape[-1])])


def kernel(x, c, w_ada, b_ada, g_pre_mix, g_post_mix, w_in, w_conv_dn, a_log_dn, dt_bias_dn, g_onorm_dn, rpb_na,
           w_out, g_pre_ffn, g_post_ffn, w_group, b_group, w_expert, b_expert, w_gate, w_up, w_down):
    bsz, s, d = x.shape
    depth = w_ada.shape[0]
    n_dn = a_log_dn.shape[2]
    dn_w = n_dn * HEAD_DIM
    for l in range(depth):
        ada = _ada(c, w_ada[l], b_ada[l])
        sh1, sc1, gt1, sh2, sc2, gt2 = [a.reshape(bsz, 1, d) for a in jnp.split(ada, 6, axis=-1)]

        wi = w_in[l]
        o_z, o_ab, o_na = 3 * dn_w, 4 * dn_w, 4 * dn_w + 4 * n_dn
        wdn = wi[:, :o_z].astype(BF16)
        wz = wi[:, o_z:o_ab].astype(BF16)
        wab = _pad_lanes(wi[:, o_ab:o_na]).astype(BF16)
        na_w = (wi.shape[1] - o_na) // 3
        wna = jnp.concatenate([wi[:, o_na:o_na + na_w] * (HEAD_DIM ** -0.5), wi[:, o_na + na_w:]], axis=1).astype(BF16)
        alog = _pad_lanes(a_log_dn[l].reshape(1, 2 * n_dn))
        dtb = _pad_lanes(dt_bias_dn[l].reshape(1, 2 * n_dn))
        dn, z, gates, na = _inproj(x, sc1, sh1, g_pre_mix[l].reshape(1, d), wdn, wz, wab, wna, alog, dtb)

        gt = gates[:, :, :32].reshape(bsz, s // DN_CHUNK, DN_CHUNK, 32).transpose(0, 1, 3, 2)
        gon2 = jnp.tile(g_onorm_dn[l].reshape(1, HEAD_DIM), (1, LANES // HEAD_DIM))
        o_dn = _delta(dn, w_conv_dn[l], gates, gt, z, gon2)

        o_na = _natten(na, _na_bias_tables(rpb_na[l]))

        wo = w_out[l].astype(BF16)
        wr = _pad_lanes(jnp.concatenate([w_group[l], w_expert[l]], axis=1))
        br = _pad_lanes(jnp.concatenate([b_group[l], b_expert[l]]).reshape(1, -1))
        x1, h2, route, counts = _outproj(o_dn, o_na, x, wo[:dn_w], wo[dn_w:], g_post_mix[l].reshape(1, d), gt1,
                                g_pre_ffn[l].reshape(1, d), sc2, sh2, wr, br)
        x = _moe(h2, route, counts, w_gate[l], w_up[l], w_down[l], x1, g_post_ffn[l].reshape(1, d), gt2)
    return x
```

```python
import numpy as np
import jax
import jax.numpy as jnp
from jax import lax
from jax.experimental import pallas as pl
from jax.experimental.pallas import tpu as pltpu

F32 = jnp.float32
BF16 = jnp.bfloat16
HIGHEST = lax.Precision.HIGHEST

GRID_W = 64
HEAD_DIM = 64
DN_CONV = 5
DN_CHUNK = 128
NA_WIN_H = 8
NA_WIN_W = 16
N_GROUPS = 4
EXPERTS_PER_GROUP = 8
N_EXPERTS = N_GROUPS * EXPERTS_PER_GROUP
EPS = 1e-6
LANES = 128
NEG_BIG = -1e30
VMEM_LIMIT = 56 * 1024 * 1024
MOE_ROW_TILE = 256


def _cparams(*sem):
    return pltpu.CompilerParams(dimension_semantics=sem, vmem_limit_bytes=VMEM_LIMIT)


def _sigmoid(x):
    return 1.0 / (1.0 + jnp.exp(-x))


def _silu(x):
    return x * _sigmoid(x)


def _softplus(x):
    return jnp.maximum(x, 0.0) + jnp.log1p(jnp.exp(-jnp.abs(x)))


def _dot(a, b, **kw):
    return jnp.dot(a, b, preferred_element_type=F32, **kw)


def _dot_nt(a, b, **kw):
    return lax.dot_general(a, b, (((1,), (1,)), ((), ())), preferred_element_type=F32, **kw)


def _dot_tn(a, b, **kw):
    return lax.dot_general(a, b, (((0,), (0,)), ((), ())), preferred_element_type=F32, **kw)


def _rms(x):
    return x * lax.rsqrt(jnp.mean(x * x, axis=-1, keepdims=True) + EPS)


def _iota2(shape):
    return lax.broadcasted_iota(jnp.int32, shape, 0), lax.broadcasted_iota(jnp.int32, shape, 1)


def _ada_kernel(c_ref, w_ref, b_ref, o_ref):
    cond = _silu(c_ref[...])
    o_ref[...] = _dot(cond, w_ref[...], precision=HIGHEST) + b_ref[...]


def _ada(c, w_ada, b_ada):
    bsz, d = c.shape
    n = w_ada.shape[1]
    tn = 1536
    return pl.pallas_call(
        _ada_kernel,
        grid=(n // tn,),
        in_specs=[pl.BlockSpec((bsz, d), lambda j: (0, 0)),
                  pl.BlockSpec((d, tn), lambda j: (0, j)),
                  pl.BlockSpec((1, tn), lambda j: (0, j))],
        out_specs=pl.BlockSpec((bsz, tn), lambda j: (0, j)),
        out_shape=jax.ShapeDtypeStruct((bsz, n), F32),
        name="ada_ln",
        compiler_params=_cparams("arbitrary"),
    )(c, w_ada, b_ada.reshape(1, n))


def _inproj_kernel(x_ref, sc_ref, sh_ref, g_ref, wdn_ref, wz_ref, wab_ref, wna_ref, alog_ref, dtb_ref,
                   dn_ref, z_ref, gate_ref, na_ref):
    h = _rms(x_ref[0]) * g_ref[...]
    h = h * (1.0 + sc_ref[0]) + sh_ref[0]
    hb = h.astype(BF16)
    dn_ref[0] = _dot(hb, wdn_ref[...]).astype(BF16)
    z_ref[0] = _dot(hb, wz_ref[...]).astype(BF16)
    na_ref[0] = _dot(hb, wna_ref[...]).astype(BF16)
    ab = _dot(hb, wab_ref[...])
    lane = lax.broadcasted_iota(jnp.int32, ab.shape, 1)
    decay = -jnp.exp(alog_ref[...]) * _softplus(ab + dtb_ref[...])
    gate_ref[0] = jnp.where(lane < 8, _chunk_cumsum(decay, reverse=False),
                            jnp.where(lane < 16, _chunk_cumsum(decay, reverse=True),
                                      _sigmoid(ab)))


def _inproj(x, sc, sh, g, wdn, wz, wab, wna, alog, dtb, tm=512):
    bsz, s, d = x.shape
    ndn, nz, nna = wdn.shape[1], wz.shape[1], wna.shape[1]
    full = lambda a: pl.BlockSpec(a.shape, lambda b, i: (0,) * a.ndim)
    mod = pl.BlockSpec((1, 1, d), lambda b, i: (b, 0, 0))
    tok = lambda n: pl.BlockSpec((1, tm, n), lambda b, i: (b, i, 0))
    return pl.pallas_call(
        _inproj_kernel,
        grid=(bsz, s // tm),
        in_specs=[tok(d), mod, mod, full(g), full(wdn), full(wz), full(wab), full(wna), full(alog), full(dtb)],
        out_specs=[tok(ndn), tok(nz), tok(LANES), tok(nna)],
        out_shape=[jax.ShapeDtypeStruct((bsz, s, ndn), BF16), jax.ShapeDtypeStruct((bsz, s, nz), BF16),
                   jax.ShapeDtypeStruct((bsz, s, LANES), F32), jax.ShapeDtypeStruct((bsz, s, nna), BF16)],
        name="prenorm_inproj",
        compiler_params=_cparams("arbitrary", "arbitrary"),
    )(x, sc, sh, g, wdn, wz, wab, wna, alog, dtb)


def _conv_silu(x, w):
    s = x.shape[0]
    row = lax.broadcasted_iota(jnp.int32, x.shape, 0)
    pad = (DN_CONV - 1) // 2
    acc = x * w[pad:pad + 1]
    for j in range(DN_CONV):
        d = j - pad
        if d == 0:
            continue
        xs = pltpu.roll(x, (-d) % s, 0)
        ok = (row + d >= 0) & (row + d < s)
        acc = acc + jnp.where(ok, xs, 0.0) * w[j:j + 1]
    return _silu(acc)


def _chunk_cumsum(g, reverse):
    s = g.shape[0]
    pos = lax.broadcasted_iota(jnp.int32, g.shape, 0) & (DN_CHUNK - 1)
    sh = 1
    while sh < DN_CHUNK:
        if reverse:
            g = g + jnp.where(pos + sh < DN_CHUNK, pltpu.roll(g, s - sh, 0), 0.0)
        else:
            g = g + jnp.where(pos >= sh, pltpu.roll(g, sh, 0), 0.0)
        sh *= 2
    return g


def _delta_kernel(q_ref, k_ref, v_ref, wq_ref, wk_ref, wv_ref, gate_ref, gt_ref, z_ref, gon_ref, o_ref,
                  k_s, q_s, kb_s, kbg_s, qd_s, kt_s, vb_s, gci_s, egl_s, m_s, t0_s, t1_s, at_s, oacc_s):
    hp = pl.program_id(1)
    s = q_ref.shape[1]
    c = DN_CHUNK
    hd = HEAD_DIM
    n_chunks = s // c

    li, lj = _iota2((LANES, LANES))
    same_head_b = lax.shift_right_logical(li, 6) == lax.shift_right_logical(lj, 6)
    same_head = same_head_b.astype(F32)

    def l2n(x):
        return x * lax.rsqrt(_dot(x * x, same_head) + EPS)

    q2 = l2n(_conv_silu(q_ref[0].astype(F32), wq_ref[...])) * (hd ** -0.5)
    k2 = l2n(_conv_silu(k_ref[0].astype(F32), wk_ref[...]))
    v2 = _conv_silu(v_ref[0].astype(F32), wv_ref[...])
    k_s[...] = k2.astype(BF16)
    q_s[...] = q2.astype(BF16)

    gates = gate_ref[0]
    lane_s = lax.broadcasted_iota(jnp.int32, gates.shape, 1)

    def pair_cols(base):
        c0 = base + 2 * hp
        col0 = jnp.sum(jnp.where(lane_s == c0, gates, 0.0), axis=-1, keepdims=True)
        col1 = jnp.sum(jnp.where(lane_s == c0 + 1, gates, 0.0), axis=-1, keepdims=True)
        return jnp.where(lane_s < hd, col0, col1)

    ri, rj = _iota2((c, c))
    eye = (ri == rj).astype(F32)

    for direction in range(2):
        gci2 = pair_cols(8 * direction)
        beta2 = pair_cols(16 + 8 * direction)
        gc3 = gci2.reshape(n_chunks, c, LANES)
        tot = gc3[:, c - 1:c, :] if direction == 0 else gc3[:, 0:1, :]
        gl2 = jnp.broadcast_to(tot, gc3.shape).reshape(s, LANES)
        egc2 = jnp.exp(gci2)
        kb2 = k2 * beta2
        kb_s[direction] = kb2.astype(BF16)
        kbg_s[direction] = (kb2 * egc2).astype(BF16)
        qd_s[direction] = (q2 * egc2).astype(BF16)
        kt_s[direction] = (k2 * jnp.exp(gl2 - gci2)).astype(BF16)
        vb_s[direction] = v2 * beta2
        gci_s[direction] = gci2
        egl_s[direction] = jnp.exp(gl2)

    lane_c = lax.broadcasted_iota(jnp.int32, (c, LANES), 1)
    head0 = lane_c < hd

    same = lambda sh: lax.shift_right_logical(ri, sh) == lax.shift_right_logical(rj, sh)
    group = 8
    n_levels = c.bit_length() - 2
    t_bufs = (t0_s, t1_s)

    def chunk_rows(g, j):
        return pl.ds(pl.multiple_of((g * group + j) * c, c), c)

    def chunk_matrices(g, carry):
        for j in range(group):
            rows = chunk_rows(g, j)
            kc = k_s[rows, :]
            qc = q_s[rows, :]
            zero = jnp.zeros_like(kc)
            for direction in range(2):
                incl = (ri >= rj) if direction == 0 else (ri <= rj)
                strict = (ri > rj) if direction == 0 else (ri < rj)
                kbc = kb_s[direction, rows, :]
                gci = gci_s[direction, rows, :]
                for hh in range(2):
                    hm = head0 if hh == 0 else jnp.logical_not(head0)
                    gi = jnp.broadcast_to(gci[:, hh * hd:hh * hd + 1], (c, c))
                    gj = jnp.broadcast_to(
                        gt_ref[0, g * group + j, pl.ds(8 * direction + 2 * hp + hh, 1), :], (c, c))
                    decay = jnp.where(incl, jnp.exp(jnp.where(incl, gi - gj, 0.0)), 0.0)
                    m = jnp.where(strict, _dot_nt(jnp.where(hm, kbc, zero), kc) * decay, 0.0)
                    m_s[direction, hh, rows, :] = m.astype(BF16)
                    t_bufs[n_levels % 2][direction, hh, rows, :] =(eye - jnp.where(same(1), m, 0.0)).astype(BF16)
                    at_s[direction, hh, rows, :] = (_dot_nt(jnp.where(hm, qc, zero), kc) * decay).astype(BF16)
        return carry

    lax.fori_loop(0, n_chunks // group, chunk_matrices, 0)

    for level in range(n_levels):
        def merge_level(g, carry, level=level):
            sh = level + 1
            src = (n_levels - level) % 2
            cmask = same(sh + 1) & jnp.logical_not(same(sh))
            for j in range(group):
                rows = chunk_rows(g, j)
                for direction in range(2):
                    for hh in range(2):
                        tb = t_bufs[src][direction, hh, rows, :]
                        mb = m_s[direction, hh, rows, :]
                        cm = jnp.where(cmask, mb, jnp.zeros_like(mb))
                        y = _dot(tb, _dot(cm, tb).astype(BF16))
                        t_bufs[1 - src][direction, hh, rows, :] =jnp.where(cmask, (-y).astype(BF16), tb)
            return carry

        lax.fori_loop(0, n_chunks // group, merge_level, 0)

    oacc_s[...] = jnp.zeros_like(oacc_s)

    def scan_step(i, states):
        new_states = []
        for direction in range(2):
            n = i if direction == 0 else n_chunks - 1 - i
            r0 = pl.multiple_of(n * c, c)
            rows = pl.ds(r0, c)
            sb = states[direction].astype(BF16)
            rhs = (vb_s[direction, rows, :] - _dot(kbg_s[direction, rows, :], sb)).astype(BF16)
            v_new = jnp.where(head0, _dot(t0_s[direction, 0, rows, :], rhs),
                              _dot(t0_s[direction, 1, rows, :], rhs)).astype(BF16)
            intra = jnp.where(head0, _dot(at_s[direction, 0, rows, :], v_new),
                              _dot(at_s[direction, 1, rows, :], v_new))
            oacc_s[rows, :] += _dot(qd_s[direction, rows, :], sb) + intra
            upd = _dot_tn(kt_s[direction, rows, :], v_new)
            new_states.append(states[direction] * egl_s[direction, pl.ds(r0, 1), :]
                              + jnp.where(same_head_b, upd, 0.0))
        return tuple(new_states)

    zero_state = jnp.zeros((LANES, LANES), F32)
    lax.fori_loop(0, n_chunks, scan_step, (zero_state, zero_state))

    o2 = oacc_s[...]
    ms = _dot(o2 * o2, same_head) * (1.0 / hd)
    o2 = o2 * lax.rsqrt(ms + EPS) * gon_ref[...]
    o_ref[0] = (o2 * _silu(z_ref[0].astype(F32))).astype(BF16)


def _delta(dn, w_conv, gates, gt, z, gon2):
    bsz, s, _ = dn.shape
    n_pairs = z.shape[2] // LANES
    c = DN_CHUNK
    col = lambda off: pl.BlockSpec((1, s, LANES), lambda b, p: (b, 0, off + p))
    wcol = lambda off: pl.BlockSpec((DN_CONV, LANES), lambda b, p: (0, off + p))
    two = lambda dt: pltpu.VMEM((2, s, LANES), dt)
    return pl.pallas_call(
        _delta_kernel,
        grid=(bsz, n_pairs),
        in_specs=[col(0), col(n_pairs), col(2 * n_pairs), wcol(0), wcol(n_pairs), wcol(2 * n_pairs),
                  pl.BlockSpec((1, s, LANES), lambda b, p: (b, 0, 0)),
                  pl.BlockSpec((1, s // c, 32, c), lambda b, p: (b, 0, 0, 0)),
                  col(0),
                  pl.BlockSpec((1, LANES), lambda b, p: (0, 0))],
        out_specs=col(0),
        out_shape=jax.ShapeDtypeStruct((bsz, s, n_pairs * LANES), BF16),
        scratch_shapes=[pltpu.VMEM((s, LANES), BF16), pltpu.VMEM((s, LANES), BF16),
                        two(BF16), two(BF16), two(BF16), two(BF16),
                        two(F32), two(F32), two(F32),
                        pltpu.VMEM((2, 2, s, LANES), BF16), pltpu.VMEM((2, 2, s, LANES), BF16),
                        pltpu.VMEM((2, 2, s, LANES), BF16), pltpu.VMEM((2, 2, s, LANES), BF16),
                        pltpu.VMEM((s, LANES), F32)],
        name="delta_rule",
        compiler_params=_cparams("arbitrary", "arbitrary"),
    )(dn, dn, dn, w_conv, w_conv, w_conv, gates, gt, z, gon2)


def _na_bias_tables(rpb):
    n_heads = rpb.shape[0]
    cc = np.arange(GRID_W)
    c0 = np.clip(cc - NA_WIN_W // 2, 0, GRID_W - NA_WIN_W)
    kc = np.arange(GRID_W)
    in_win = (kc[None, :] >= c0[:, None]) & (kc[None, :] < c0[:, None] + NA_WIN_W)
    col_off = kc[None, :] - cc[:, None] + NA_WIN_W - 1
    onehot = (col_off[None] == np.arange(2 * NA_WIN_W - 1)[:, None, None]) & in_win[None]
    rows = jnp.stack([rpb[:, NA_WIN_H - 1 - d:2 * NA_WIN_H - 1 - d, :] for d in range(NA_WIN_H)], axis=1)
    rows = rows.reshape(n_heads // 2, 2, NA_WIN_H, NA_WIN_H, 2 * NA_WIN_W - 1).astype(F32)
    bias = jnp.einsum("phdic,cqk->pdhqik", rows, jnp.asarray(onehot, F32), precision=HIGHEST)
    bias = bias + jnp.asarray(np.where(in_win, 0.0, NEG_BIG), F32)[None, None, None, :, None, :]
    return bias.reshape(n_heads // 2, NA_WIN_H, 2 * GRID_W, NA_WIN_H * GRID_W)


def _natten_kernel(q_ref, k_ref, v_ref, bias_ref, o_ref):
    s = q_ref.shape[1]
    rows = s // GRID_W
    w = GRID_W
    win = min(NA_WIN_H, rows)
    hd = HEAD_DIM
    lane = lax.broadcasted_iota(jnp.int32, (w, LANES), 1)
    first = lane < hd
    for r in range(rows):
        r0 = min(max(r - win // 2, 0), rows - win)
        d = r - r0
        q = q_ref[0, r * w:(r + 1) * w, :]
        zero = jnp.zeros_like(q)
        q_st = jnp.concatenate([jnp.where(first, q, zero), jnp.where(first, zero, q)], axis=0)
        kw = k_ref[0, r0 * w:(r0 + win) * w, :]
        vw = v_ref[0, r0 * w:(r0 + win) * w, :]
        sc = _dot_nt(q_st, kw) + bias_ref[0, d]
        sc = sc - jnp.max(sc, axis=-1, keepdims=True)
        p = jnp.exp(sc)
        denom = jnp.sum(p, axis=-1, keepdims=True)
        o = _dot(p.astype(BF16), vw) / denom
        o_ref[0, r * w:(r + 1) * w, :] = jnp.where(first, o[:w], o[w:]).astype(BF16)


def _natten(na, bias):
    bsz, s, n3 = na.shape
    n_pairs = n3 // (3 * LANES)
    col = lambda off: pl.BlockSpec((1, s, LANES), lambda p, b: (b, 0, off + p))
    return pl.pallas_call(
        _natten_kernel,
        grid=(n_pairs, bsz),
        in_specs=[col(0), col(n_pairs), col(2 * n_pairs),
                  pl.BlockSpec((1,) + bias.shape[1:], lambda p, b: (p, 0, 0, 0))],
        out_specs=col(0),
        out_shape=jax.ShapeDtypeStruct((bsz, s, n_pairs * LANES), BF16),
        name="natten",
        compiler_params=_cparams("arbitrary", "arbitrary"),
    )(na, na, na, bias)


def _outproj_kernel(odn_ref, ona_ref, x_ref, wo_dn_ref, wo_na_ref, gpost_ref, gt1_ref, gpre_ref, sc_ref, sh_ref,
                    wr_ref, br_ref, x1_ref, h_ref, route_ref, cnt_ref, cnt_s):
    y = _dot(odn_ref[0], wo_dn_ref[...]) + _dot(ona_ref[0], wo_na_ref[...])
    x1 = x_ref[0] + gt1_ref[0] * (_rms(y) * gpost_ref[...])
    x1_ref[0] = x1
    h = _rms(x1) * gpre_ref[...]
    h = h * (1.0 + sc_ref[0]) + sh_ref[0]
    h_ref[0] = h

    logits = _dot(h, wr_ref[...], precision=HIGHEST) + br_ref[...]
    lane = lax.broadcasted_iota(jnp.int32, logits.shape, 1).astype(F32)
    big = float(LANES)

    def masked_softmax(mask):
        lm = jnp.where(mask, logits, NEG_BIG)
        e = jnp.where(mask, jnp.exp(lm - jnp.max(lm, axis=-1, keepdims=True)), 0.0)
        return e / jnp.sum(e, axis=-1, keepdims=True)

    def top1(p, mask):
        pm = jnp.where(mask, p, -1.0)
        best = jnp.max(pm, axis=-1, keepdims=True)
        idx = jnp.min(jnp.where(mask & (pm == best), lane, big), axis=-1, keepdims=True)
        return best, idx

    gmask = lane < N_GROUPS
    gp_top, g_idx = top1(masked_softmax(gmask), gmask)
    e_lo = N_GROUPS + g_idx * EXPERTS_PER_GROUP
    emask = (lane >= e_lo) & (lane < e_lo + EXPERTS_PER_GROUP)
    pe = masked_softmax(emask)
    p1, i1 = top1(pe, emask)
    p2, i2 = top1(pe, emask & (lane != i1))
    w1 = gp_top * p1 / (p1 + p2)
    w2 = gp_top * p2 / (p1 + p2)

    @pl.when((pl.program_id(0) == 0) & (pl.program_id(1) == 0))
    def _():
        cnt_s[...] = jnp.zeros_like(cnt_s)

    el = lane + N_GROUPS
    oh1 = el == i1
    oh2 = el == i2
    tm = logits.shape[0]
    ti, tj = _iota2((tm, tm))
    before = (ti > tj).astype(BF16)
    pre1 = _dot(before, oh1.astype(BF16))
    pre2 = _dot(before, oh2.astype(BF16))
    tot1 = jnp.sum(oh1.astype(F32), axis=0, keepdims=True)
    tot2 = jnp.sum(oh2.astype(F32), axis=0, keepdims=True)
    base = cnt_s[...]
    rank1 = jnp.sum(jnp.where(oh1, base + pre1, 0.0), axis=-1, keepdims=True)
    rank2 = jnp.sum(jnp.where(oh2, base + tot1 + pre2, 0.0), axis=-1, keepdims=True)
    cnt_s[...] = base + tot1 + tot2
    cnt_ref[...] = cnt_s[...]
    fields = (i1 - N_GROUPS, i2 - N_GROUPS, w1, w2, rank1, rank2)
    route = jnp.zeros_like(logits)
    for n, f in enumerate(fields):
        route = jnp.where(lane == n, f, route)
    route_ref[0] = route


def _outproj(odn, ona, x, wo_dn, wo_na, gpost, gt1, gpre, sc2, sh2, wr, br, tm=512):
    bsz, s, d = x.shape
    full = lambda a: pl.BlockSpec(a.shape, lambda b, i: (0,) * a.ndim)
    mod = pl.BlockSpec((1, 1, d), lambda b, i: (b, 0, 0))
    tok = lambda n: pl.BlockSpec((1, tm, n), lambda b, i: (b, i, 0))
    return pl.pallas_call(
        _outproj_kernel,
        grid=(bsz, s // tm),
        in_specs=[tok(odn.shape[2]), tok(ona.shape[2]), tok(d), full(wo_dn), full(wo_na), full(gpost), mod,
                  full(gpre), mod, mod, full(wr), full(br)],
        out_specs=[tok(d), tok(d), tok(LANES), pl.BlockSpec((1, LANES), lambda b, i: (0, 0))],
        out_shape=[jax.ShapeDtypeStruct((bsz, s, d), F32), jax.ShapeDtypeStruct((bsz, s, d), F32),
                   jax.ShapeDtypeStruct((bsz, s, LANES), F32), jax.ShapeDtypeStruct((1, LANES), F32)],
        scratch_shapes=[pltpu.VMEM((1, LANES), F32)],
        name="outproj_route",
        compiler_params=_cparams("arbitrary", "arbitrary"),
    )(odn, ona, x, wo_dn, wo_na, gpost, gt1, gpre, sc2, sh2, wr, br)


def _dispatch_kernel(pos_ref, h_ref, xs_in, xs_hbm, sem):
    del xs_in
    tb = pos_ref.shape[2]

    def issue(r, carry):
        src = h_ref.at[pl.ds(r, 1)]
        for slot in range(2):
            pltpu.make_async_copy(src, xs_hbm.at[pl.ds(pos_ref[0, slot, r], 1)], sem).start()
        return carry

    lax.fori_loop(0, tb, issue, 0, unroll=8)
    for _ in range(2):
        pltpu.make_async_copy(h_ref, xs_hbm.at[pl.ds(0, tb)], sem).wait()


def _dispatch(h, pos, n_rows, tb=512):
    t, d = h.shape
    return pl.pallas_call(
        _dispatch_kernel,
        grid=(t // tb,),
        in_specs=[pl.BlockSpec((1, 2, tb), lambda i: (i, 0, 0), memory_space=pltpu.SMEM),
                  pl.BlockSpec((tb, d), lambda i: (i, 0)),
                  pl.BlockSpec(memory_space=pl.ANY)],
        out_specs=pl.BlockSpec(memory_space=pl.ANY),
        out_shape=jax.ShapeDtypeStruct((n_rows, d), h.dtype),
        scratch_shapes=[pltpu.SemaphoreType.DMA(())],
        input_output_aliases={2: 0},
        name="moe_dispatch",
        compiler_params=_cparams("arbitrary"),
    )(pos, h, jnp.zeros((n_rows, d), h.dtype))


def _expert_kernel(te_ref, ts_ref, nu_ref, x_ref, wg_ref, wu_ref, wd_ref, y_ref, wg_s, wu_s, wd_s):
    j = pl.program_id(0)
    valid = j < nu_ref[0]
    new_expert = (j == 0) | (te_ref[j] != te_ref[jnp.maximum(j - 1, 0)])

    @pl.when(valid & new_expert)
    def _():
        wg_s[...] = wg_ref[0].astype(BF16)
        wu_s[...] = wu_ref[0].astype(BF16)
        wd_s[...] = wd_ref[0].astype(BF16)

    @pl.when(valid)
    def _():
        x = x_ref[...].astype(BF16)
        he = (_silu(_dot(x, wg_s[...])) * _dot(x, wu_s[...])).astype(BF16)
        y_ref[...] = _dot(he, wd_s[...])

    @pl.when(jnp.logical_not(valid))
    def _():
        y_ref[...] = jnp.zeros_like(y_ref)


def _experts(xs, tile_expert, tile_src, n_used, w_gate, w_up, w_down):
    n_rows, d = xs.shape
    n_exp, _, de = w_gate.shape
    n_tiles = n_rows // MOE_ROW_TILE
    rows = pl.BlockSpec((MOE_ROW_TILE, d), lambda j, te, ts, nu: (ts[j], 0))
    grid_spec = pltpu.PrefetchScalarGridSpec(
        num_scalar_prefetch=3,
        grid=(n_tiles,),
        in_specs=[rows,
                  pl.BlockSpec((1, d, de), lambda j, te, ts, nu: (te[j], 0, 0)),
                  pl.BlockSpec((1, d, de), lambda j, te, ts, nu: (te[j], 0, 0)),
                  pl.BlockSpec((1, de, d), lambda j, te, ts, nu: (te[j], 0, 0))],
        out_specs=pl.BlockSpec((MOE_ROW_TILE, d), lambda j, te, ts, nu: (j, 0)),
        scratch_shapes=[pltpu.VMEM((d, de), BF16), pltpu.VMEM((d, de), BF16), pltpu.VMEM((de, d), BF16)],
    )
    return pl.pallas_call(
        _expert_kernel,
        grid_spec=grid_spec,
        out_shape=jax.ShapeDtypeStruct((n_rows, d), F32),
        name="moe_experts",
        compiler_params=_cparams("arbitrary"),
    )(tile_expert, tile_src, n_used, xs, w_gate, w_up, w_down)


def _combine_kernel(pos_ref, posn_ref, route_ref, x1_ref, gpost_ref, gt2_ref, ys_hbm, o_ref, buf, sems):
    i = pl.program_id(0)
    n = pl.num_programs(0)
    tc = pos_ref.shape[2]

    def start_tile(p_ref, slot):
        def issue(r, carry):
            for k in range(2):
                pltpu.make_async_copy(ys_hbm.at[pl.ds(p_ref[0, k, r], 1)], buf.at[slot, k, pl.ds(r, 1)],
                                      sems.at[slot]).start()
            return carry
        lax.fori_loop(0, tc, issue, 0, unroll=8)

    @pl.when(i == 0)
    def _():
        start_tile(pos_ref, 0)

    @pl.when(i + 1 < n)
    def _():
        start_tile(posn_ref, (i + 1) % 2)

    slot = i % 2
    pltpu.make_async_copy(buf.at[slot], buf.at[slot], sems.at[slot]).wait()
    route = route_ref[...]
    lane = lax.broadcasted_iota(jnp.int32, route.shape, 1)
    w1 = jnp.sum(jnp.where(lane == 2, route, 0.0), axis=-1, keepdims=True)
    w2 = jnp.sum(jnp.where(lane == 3, route, 0.0), axis=-1, keepdims=True)
    y = w1 * buf[slot, 0] + w2 * buf[slot, 1]
    o_ref[...] = x1_ref[...] + gt2_ref[0] * (_rms(y) * gpost_ref[...])


def _combine(ys, pos, route, x1, gpost, gt2, seq, tc=256):
    t, d = x1.shape
    n = t // tc
    per_seq = seq // tc
    smem = lambda f: pl.BlockSpec((1, 2, tc), f, memory_space=pltpu.SMEM)
    return pl.pallas_call(
        _combine_kernel,
        grid=(n,),
        in_specs=[smem(lambda i: (i, 0, 0)), smem(lambda i: (jnp.minimum(i + 1, n - 1), 0, 0)),
                  pl.BlockSpec((tc, LANES), lambda i: (i, 0)),
                  pl.BlockSpec((tc, d), lambda i: (i, 0)),
                  pl.BlockSpec((1, d), lambda i: (0, 0)),
                  pl.BlockSpec((1, 1, d), lambda i: (i // per_seq, 0, 0)),
                  pl.BlockSpec(memory_space=pl.ANY)],
        out_specs=pl.BlockSpec((tc, d), lambda i: (i, 0)),
        out_shape=jax.ShapeDtypeStruct((t, d), F32),
        scratch_shapes=[pltpu.VMEM((2, 2, tc, d), F32), pltpu.SemaphoreType.DMA((2,))],
        name="moe_combine",
        compiler_params=_cparams("arbitrary"),
    )(pos, pos, route, x1, gpost, gt2, ys)


def _moe(h, route, counts, w_gate, w_up, w_down, x1, gpost, gt2):
    bsz, s, d = x1.shape
    t = bsz * s
    n_exp = w_gate.shape[0]
    tr = MOE_ROW_TILE
    n_tiles = 2 * t // tr + n_exp
    route = route.reshape(t, LANES)
    ids = route[:, 0:2].astype(jnp.int32)
    ranks = route[:, 4:6].astype(jnp.int32)
    cnt = counts[0, :n_exp].astype(jnp.int32)
    padded = (cnt + tr - 1) // tr * tr
    ends = jnp.cumsum(padded)
    offs = ends - padded
    onehot = ids[:, :, None] == jnp.arange(n_exp, dtype=jnp.int32)[None, None, :]
    pos = jnp.sum(jnp.where(onehot, offs[None, None, :], 0), axis=-1) + ranks
    n_used = ends[-1] // tr
    tile = jnp.arange(n_tiles, dtype=jnp.int32)
    tile_expert = jnp.sum((tile[:, None] >= (ends // tr)[None, :]).astype(jnp.int32), axis=-1)
    last = n_used - 1
    valid = tile < n_used
    tile_src = jnp.maximum(jnp.where(valid, tile, last), 0)
    tile_expert = jnp.minimum(jnp.where(valid, tile_expert, jnp.sum((last >= ends // tr).astype(jnp.int32))),
                              n_exp - 1)

    def tiles_of(tok_tile):
        return pos.reshape(t // tok_tile, tok_tile, 2).transpose(0, 2, 1)

    xs = _dispatch(h.reshape(t, d), tiles_of(512), n_tiles * tr)
    ys = _experts(xs, tile_expert, tile_src, n_used.reshape(1), w_gate, w_up, w_down)
    out = _combine(ys, tiles_of(256), route, x1.reshape(t, d), gpost, gt2, s)
    return out.reshape(bsz, s, d)


def _pad_lanes(a, n=LANES):
    return jnp.pad(a, [(0, 0)] * (a.ndim - 1) + [(0, n - a.shape[-1])])


def kernel(x, c, w_ada, b_ada, g_pre_mix, g_post_mix, w_in, w_conv_dn, a_log_dn, dt_bias_dn, g_onorm_dn, rpb_na,
           w_out, g_pre_ffn, g_post_ffn, w_group, b_group, w_expert, b_expert, w_gate, w_up, w_down):
    bsz, s, d = x.shape
    depth = w_ada.shape[0]
    n_dn = a_log_dn.shape[2]
    dn_w = n_dn * HEAD_DIM
    for l in range(depth):
        ada = _ada(c, w_ada[l], b_ada[l])
        sh1, sc1, gt1, sh2, sc2, gt2 = [a.reshape(bsz, 1, d) for a in jnp.split(ada, 6, axis=-1)]

        wi = w_in[l]
        o_z, o_ab, o_na = 3 * dn_w, 4 * dn_w, 4 * dn_w + 4 * n_dn
        wdn = wi[:, :o_z].astype(BF16)
        wz = wi[:, o_z:o_ab].astype(BF16)
        wab = _pad_lanes(wi[:, o_ab:o_na]).astype(BF16)
        na_w = (wi.shape[1] - o_na) // 3
        wna = jnp.concatenate([wi[:, o_na:o_na + na_w] * (HEAD_DIM ** -0.5), wi[:, o_na + na_w:]], axis=1).astype(BF16)
        alog = _pad_lanes(a_log_dn[l].reshape(1, 2 * n_dn))
        dtb = _pad_lanes(dt_bias_dn[l].reshape(1, 2 * n_dn))
        dn, z, gates, na = _inproj(x, sc1, sh1, g_pre_mix[l].reshape(1, d), wdn, wz, wab, wna, alog, dtb)

        gt = gates[:, :, :32].reshape(bsz, s // DN_CHUNK, DN_CHUNK, 32).transpose(0, 1, 3, 2)
        gon2 = jnp.tile(g_onorm_dn[l].reshape(1, HEAD_DIM), (1, LANES // HEAD_DIM))
        o_dn = _delta(dn, w_conv_dn[l], gates, gt, z, gon2)

        o_na = _natten(na, _na_bias_tables(rpb_na[l]))

        wo = w_out[l].astype(BF16)
        wr = _pad_lanes(jnp.concatenate([w_group[l], w_expert[l]], axis=1))
        br = _pad_lanes(jnp.concatenate([b_group[l], b_expert[l]]).reshape(1, -1))
        x1, h2, route, counts = _outproj(o_dn, o_na, x, wo[:dn_w], wo[dn_w:], g_post_mix[l].reshape(1, d), gt1,
                                g_pre_ffn[l].reshape(1, d), sc2, sh2, wr, br)
        x = _moe(h2, route, counts, w_gate[l], w_up[l], w_down[l], x1, g_post_ffn[l].reshape(1, d), gt2)
    return x
```

```python
import numpy as np
import jax
import jax.numpy as jnp
from jax import lax
from jax.experimental import pallas as pl
from jax.experimental.pallas import tpu as pltpu

F32 = jnp.float32
BF16 = jnp.bfloat16
HIGHEST = lax.Precision.HIGHEST

GRID_W = 64
HEAD_DIM = 64
DN_CONV = 5
DN_CHUNK = 128
NA_WIN_H = 8
NA_WIN_W = 16
N_GROUPS = 4
EXPERTS_PER_GROUP = 8
N_EXPERTS = N_GROUPS * EXPERTS_PER_GROUP
EPS = 1e-6
LANES = 128
NEG_BIG = -1e30
VMEM_LIMIT = 56 * 1024 * 1024
MOE_ROW_TILE = 256


def _cparams(*sem):
    return pltpu.CompilerParams(dimension_semantics=sem, vmem_limit_bytes=VMEM_LIMIT)


def _sigmoid(x):
    return 1.0 / (1.0 + jnp.exp(-x))


def _silu(x):
    return x * _sigmoid(x)


def _softplus(x):
    return jnp.maximum(x, 0.0) + jnp.log1p(jnp.exp(-jnp.abs(x)))


def _dot(a, b, **kw):
    return jnp.dot(a, b, preferred_element_type=F32, **kw)


def _dot_nt(a, b, **kw):
    return lax.dot_general(a, b, (((1,), (1,)), ((), ())), preferred_element_type=F32, **kw)


def _dot_tn(a, b, **kw):
    return lax.dot_general(a, b, (((0,), (0,)), ((), ())), preferred_element_type=F32, **kw)


def _rms(x):
    return x * lax.rsqrt(jnp.mean(x * x, axis=-1, keepdims=True) + EPS)


def _iota2(shape):
    return lax.broadcasted_iota(jnp.int32, shape, 0), lax.broadcasted_iota(jnp.int32, shape, 1)


def _ada_kernel(c_ref, w_ref, b_ref, o_ref):
    cond = _silu(c_ref[...])
    o_ref[...] = _dot(cond, w_ref[...], precision=HIGHEST) + b_ref[...]


def _ada(c, w_ada, b_ada):
    bsz, d = c.shape
    n = w_ada.shape[1]
    tn = 1536
    return pl.pallas_call(
        _ada_kernel,
        grid=(n // tn,),
        in_specs=[pl.BlockSpec((bsz, d), lambda j: (0, 0)),
                  pl.BlockSpec((d, tn), lambda j: (0, j)),
                  pl.BlockSpec((1, tn), lambda j: (0, j))],
        out_specs=pl.BlockSpec((bsz, tn), lambda j: (0, j)),
        out_shape=jax.ShapeDtypeStruct((bsz, n), F32),
        name="ada_ln",
        compiler_params=_cparams("arbitrary"),
    )(c, w_ada, b_ada.reshape(1, n))


def _inproj_kernel(x_ref, sc_ref, sh_ref, g_ref, wdn_ref, wz_ref, wab_ref, wna_ref, alog_ref, dtb_ref,
                   dn_ref, z_ref, gate_ref, na_ref):
    h = _rms(x_ref[0]) * g_ref[...]
    h = h * (1.0 + sc_ref[0]) + sh_ref[0]
    hb = h.astype(BF16)
    dn_ref[0] = _dot(hb, wdn_ref[...]).astype(BF16)
    z_ref[0] = _dot(hb, wz_ref[...]).astype(BF16)
    na_ref[0] = _dot(hb, wna_ref[...]).astype(BF16)
    ab = _dot(hb, wab_ref[...])
    lane = lax.broadcasted_iota(jnp.int32, ab.shape, 1)
    decay = -jnp.exp(alog_ref[...]) * _softplus(ab + dtb_ref[...])
    gate_ref[0] = jnp.where(lane < 8, _chunk_cumsum(decay, reverse=False),
                            jnp.where(lane < 16, _chunk_cumsum(decay, reverse=True),
                                      _sigmoid(ab)))


def _inproj(x, sc, sh, g, wdn, wz, wab, wna, alog, dtb, tm=512):
    bsz, s, d = x.shape
    ndn, nz, nna = wdn.shape[1], wz.shape[1], wna.shape[1]
    full = lambda a: pl.BlockSpec(a.shape, lambda b, i: (0,) * a.ndim)
    mod = pl.BlockSpec((1, 1, d), lambda b, i: (b, 0, 0))
    tok = lambda n: pl.BlockSpec((1, tm, n), lambda b, i: (b, i, 0))
    return pl.pallas_call(
        _inproj_kernel,
        grid=(bsz, s // tm),
        in_specs=[tok(d), mod, mod, full(g), full(wdn), full(wz), full(wab), full(wna), full(alog), full(dtb)],
        out_specs=[tok(ndn), tok(nz), tok(LANES), tok(nna)],
        out_shape=[jax.ShapeDtypeStruct((bsz, s, ndn), BF16), jax.ShapeDtypeStruct((bsz, s, nz), BF16),
                   jax.ShapeDtypeStruct((bsz, s, LANES), F32), jax.ShapeDtypeStruct((bsz, s, nna), BF16)],
        name="prenorm_inproj",
        compiler_params=_cparams("arbitrary", "arbitrary"),
    )(x, sc, sh, g, wdn, wz, wab, wna, alog, dtb)


def _conv_silu(x, w):
    s = x.shape[0]
    row = lax.broadcasted_iota(jnp.int32, x.shape, 0)
    pad = (DN_CONV - 1) // 2
    acc = x * w[pad:pad + 1]
    for j in range(DN_CONV):
        d = j - pad
        if d == 0:
            continue
        xs = pltpu.roll(x, (-d) % s, 0)
        ok = (row + d >= 0) & (row + d < s)
        acc = acc + jnp.where(ok, xs, 0.0) * w[j:j + 1]
    return _silu(acc)


def _chunk_cumsum(g, reverse):
    s = g.shape[0]
    pos = lax.broadcasted_iota(jnp.int32, g.shape, 0) & (DN_CHUNK - 1)
    sh = 1
    while sh < DN_CHUNK:
        if reverse:
            g = g + jnp.where(pos + sh < DN_CHUNK, pltpu.roll(g, s - sh, 0), 0.0)
        else:
            g = g + jnp.where(pos >= sh, pltpu.roll(g, sh, 0), 0.0)
        sh *= 2
    return g


def _delta_kernel(q_ref, k_ref, v_ref, wq_ref, wk_ref, wv_ref, gate_ref, gt_ref, z_ref, gon_ref, o_ref,
                  k_s, q_s, kb_s, kbg_s, qd_s, kt_s, vb_s, gci_s, egl_s, m_s, t0_s, t1_s, at_s, oacc_s):
    hp = pl.program_id(1)
    s = q_ref.shape[1]
    c = DN_CHUNK
    hd = HEAD_DIM
    n_chunks = s // c

    li, lj = _iota2((LANES, LANES))
    same_head_b = lax.shift_right_logical(li, 6) == lax.shift_right_logical(lj, 6)
    same_head = same_head_b.astype(F32)

    def l2n(x):
        return x * lax.rsqrt(_dot(x * x, same_head) + EPS)

    q2 = l2n(_conv_silu(q_ref[0].astype(F32), wq_ref[...])) * (hd ** -0.5)
    k2 = l2n(_conv_silu(k_ref[0].astype(F32), wk_ref[...]))
    v2 = _conv_silu(v_ref[0].astype(F32), wv_ref[...])
    k_s[...] = k2.astype(BF16)
    q_s[...] = q2.astype(BF16)

    gates = gate_ref[0]
    lane_s = lax.broadcasted_iota(jnp.int32, gates.shape, 1)

    def pair_cols(base):
        c0 = base + 2 * hp
        col0 = jnp.sum(jnp.where(lane_s == c0, gates, 0.0), axis=-1, keepdims=True)
        col1 = jnp.sum(jnp.where(lane_s == c0 + 1, gates, 0.0), axis=-1, keepdims=True)
        return jnp.where(lane_s < hd, col0, col1)

    ri, rj = _iota2((c, c))
    eye = (ri == rj).astype(F32)

    for direction in range(2):
        gci2 = pair_cols(8 * direction)
        beta2 = pair_cols(16 + 8 * direction)
        gc3 = gci2.reshape(n_chunks, c, LANES)
        tot = gc3[:, c - 1:c, :] if direction == 0 else gc3[:, 0:1, :]
        gl2 = jnp.broadcast_to(tot, gc3.shape).reshape(s, LANES)
        egc2 = jnp.exp(gci2)
        kb2 = k2 * beta2
        kb_s[direction] = kb2.astype(BF16)
        kbg_s[direction] = (kb2 * egc2).astype(BF16)
        qd_s[direction] = (q2 * egc2).astype(BF16)
        kt_s[direction] = (k2 * jnp.exp(gl2 - gci2)).astype(BF16)
        vb_s[direction] = v2 * beta2
        gci_s[direction] = gci2
        egl_s[direction] = jnp.exp(gl2)

    lane_c = lax.broadcasted_iota(jnp.int32, (c, LANES), 1)
    head0 = lane_c < hd

    same = lambda sh: lax.shift_right_logical(ri, sh) == lax.shift_right_logical(rj, sh)
    group = 8
    n_levels = c.bit_length() - 2
    t_bufs = (t0_s, t1_s)

    def chunk_rows(g, j):
        return pl.ds(pl.multiple_of((g * group + j) * c, c), c)

    def chunk_matrices(g, carry):
        for j in range(group):
            rows = chunk_rows(g, j)
            kc = k_s[rows, :]
            qc = q_s[rows, :]
            zero = jnp.zeros_like(kc)
            for direction in range(2):
                incl = (ri >= rj) if direction == 0 else (ri <= rj)
                strict = (ri > rj) if direction == 0 else (ri < rj)
                kbc = kb_s[direction, rows, :]
                gci = gci_s[direction, rows, :]
                for hh in range(2):
                    hm = head0 if hh == 0 else jnp.logical_not(head0)
                    gi = jnp.broadcast_to(gci[:, hh * hd:hh * hd + 1], (c, c))
                    gj = jnp.broadcast_to(
                        gt_ref[0, g * group + j, pl.ds(8 * direction + 2 * hp + hh, 1), :], (c, c))
                    decay = jnp.where(incl, jnp.exp(jnp.where(incl, gi - gj, 0.0)), 0.0)
                    m = jnp.where(strict, _dot_nt(jnp.where(hm, kbc, zero), kc) * decay, 0.0)
                    m_s[direction, hh, rows, :] = m.astype(BF16)
                    t_bufs[n_levels % 2][direction, hh, rows, :] =(eye - jnp.where(same(1), m, 0.0)).astype(BF16)
                    at_s[direction, hh, rows, :] = (_dot_nt(jnp.where(hm, qc, zero), kc) * decay).astype(BF16)
        return carry

    lax.fori_loop(0, n_chunks // group, chunk_matrices, 0)

    for level in range(n_levels):
        def merge_level(g, carry, level=level):
            sh = level + 1
            src = (n_levels - level) % 2
            cmask = same(sh + 1) & jnp.logical_not(same(sh))
            for j in range(group):
                rows = chunk_rows(g, j)
                for direction in range(2):
                    for hh in range(2):
                        tb = t_bufs[src][direction, hh, rows, :]
                        mb = m_s[direction, hh, rows, :]
                        cm = jnp.where(cmask, mb, jnp.zeros_like(mb))
                        y = _dot(tb, _dot(cm, tb).astype(BF16))
                        t_bufs[1 - src][direction, hh, rows, :] =jnp.where(cmask, (-y).astype(BF16), tb)
            return carry

        lax.fori_loop(0, n_chunks // group, merge_level, 0)

    oacc_s[...] = jnp.zeros_like(oacc_s)

    def scan_step(i, states):
        new_states = []
        for direction in range(2):
            n = i if direction == 0 else n_chunks - 1 - i
            r0 = pl.multiple_of(n * c, c)
            rows = pl.ds(r0, c)
            sb = states[direction].astype(BF16)
            rhs = (vb_s[direction, rows, :] - _dot(kbg_s[direction, rows, :], sb)).astype(BF16)
            v_new = jnp.where(head0, _dot(t0_s[direction, 0, rows, :], rhs),
                              _dot(t0_s[direction, 1, rows, :], rhs)).astype(BF16)
            intra = jnp.where(head0, _dot(at_s[direction, 0, rows, :], v_new),
                              _dot(at_s[direction, 1, rows, :], v_new))
            oacc_s[rows, :] += _dot(qd_s[direction, rows, :], sb) + intra
            upd = _dot_tn(kt_s[direction, rows, :], v_new)
            new_states.append(states[direction] * egl_s[direction, pl.ds(r0, 1), :]
                              + jnp.where(same_head_b, upd, 0.0))
        return tuple(new_states)

    zero_state = jnp.zeros((LANES, LANES), F32)
    lax.fori_loop(0, n_chunks, scan_step, (zero_state, zero_state))

    o2 = oacc_s[...]
    ms = _dot(o2 * o2, same_head) * (1.0 / hd)
    o2 = o2 * lax.rsqrt(ms + EPS) * gon_ref[...]
    o_ref[0] = (o2 * _silu(z_ref[0].astype(F32))).astype(BF16)


def _delta(dn, w_conv, gates, gt, z, gon2):
    bsz, s, _ = dn.shape
    n_pairs = z.shape[2] // LANES
    c = DN_CHUNK
    col = lambda off: pl.BlockSpec((1, s, LANES), lambda b, p: (b, 0, off + p))
    wcol = lambda off: pl.BlockSpec((DN_CONV, LANES), lambda b, p: (0, off + p))
    two = lambda dt: pltpu.VMEM((2, s, LANES), dt)
    return pl.pallas_call(
        _delta_kernel,
        grid=(bsz, n_pairs),
        in_specs=[col(0), col(n_pairs), col(2 * n_pairs), wcol(0), wcol(n_pairs), wcol(2 * n_pairs),
                  pl.BlockSpec((1, s, LANES), lambda b, p: (b, 0, 0)),
                  pl.BlockSpec((1, s // c, 32, c), lambda b, p: (b, 0, 0, 0)),
                  col(0),
                  pl.BlockSpec((1, LANES), lambda b, p: (0, 0))],
        out_specs=col(0),
        out_shape=jax.ShapeDtypeStruct((bsz, s, n_pairs * LANES), BF16),
        scratch_shapes=[pltpu.VMEM((s, LANES), BF16), pltpu.VMEM((s, LANES), BF16),
                        two(BF16), two(BF16), two(BF16), two(BF16),
                        two(F32), two(F32), two(F32),
                        pltpu.VMEM((2, 2, s, LANES), BF16), pltpu.VMEM((2, 2, s, LANES), BF16),
                        pltpu.VMEM((2, 2, s, LANES), BF16), pltpu.VMEM((2, 2, s, LANES), BF16),
                        pltpu.VMEM((s, LANES), F32)],
        name="delta_rule",
        compiler_params=_cparams("arbitrary", "arbitrary"),
    )(dn, dn, dn, w_conv, w_conv, w_conv, gates, gt, z, gon2)


def _na_bias_tables(rpb):
    n_heads = rpb.shape[0]
    cc = np.arange(GRID_W)
    c0 = np.clip(cc - NA_WIN_W // 2, 0, GRID_W - NA_WIN_W)
    kc = np.arange(GRID_W)
    in_win = (kc[None, :] >= c0[:, None]) & (kc[None, :] < c0[:, None] + NA_WIN_W)
    col_off = kc[None, :] - cc[:, None] + NA_WIN_W - 1
    onehot = (col_off[None] == np.arange(2 * NA_WIN_W - 1)[:, None, None]) & in_win[None]
    rows = jnp.stack([rpb[:, NA_WIN_H - 1 - d:2 * NA_WIN_H - 1 - d, :] for d in range(NA_WIN_H)], axis=1)
    rows = rows.reshape(n_heads // 2, 2, NA_WIN_H, NA_WIN_H, 2 * NA_WIN_W - 1).astype(F32)
    bias = jnp.einsum("phdic,cqk->pdhqik", rows, jnp.asarray(onehot, F32), precision=HIGHEST)
    bias = bias + jnp.asarray(np.where(in_win, 0.0, NEG_BIG), F32)[None, None, None, :, None, :]
    return bias.reshape(n_heads // 2, NA_WIN_H, 2 * GRID_W, NA_WIN_H * GRID_W)


def _natten_kernel(q_ref, k_ref, v_ref, bias_ref, o_ref, sc_s, p_s):
    s = q_ref.shape[1]
    rows = s // GRID_W
    w = GRID_W
    win = min(NA_WIN_H, rows)
    hd = HEAD_DIM
    lane = lax.broadcasted_iota(jnp.int32, (w, LANES), 1)
    first = lane < hd
    window = lambda r: min(max(r - win // 2, 0), rows - win)
    for r in range(rows):
        r0 = window(r)
        q = q_ref[0, r * w:(r + 1) * w, :]
        zero = jnp.zeros_like(q)
        q_st = jnp.concatenate([jnp.where(first, q, zero), jnp.where(first, zero, q)], axis=0)
        kw = k_ref[0, r0 * w:(r0 + win) * w, :]
        sc_s[r] = _dot_nt(q_st, kw) + bias_ref[0, r - r0]
    for r in range(rows):
        sc = sc_s[r]
        p = jnp.exp(sc - jnp.max(sc, axis=-1, keepdims=True))
        p_s[r] = (p * (1.0 / jnp.sum(p, axis=-1, keepdims=True))).astype(BF16)
    for r in range(rows):
        r0 = window(r)
        o = _dot(p_s[r], v_ref[0, r0 * w:(r0 + win) * w, :])
        o_ref[0, r * w:(r + 1) * w, :] = jnp.where(first, o[:w], o[w:]).astype(BF16)


def _natten(na, bias):
    bsz, s, n3 = na.shape
    n_pairs = n3 // (3 * LANES)
    col = lambda off: pl.BlockSpec((1, s, LANES), lambda p, b: (b, 0, off + p))
    return pl.pallas_call(
        _natten_kernel,
        grid=(n_pairs, bsz),
        in_specs=[col(0), col(n_pairs), col(2 * n_pairs),
                  pl.BlockSpec((1,) + bias.shape[1:], lambda p, b: (p, 0, 0, 0))],
        out_specs=col(0),
        out_shape=jax.ShapeDtypeStruct((bsz, s, n_pairs * LANES), BF16),
        scratch_shapes=[pltpu.VMEM((s // GRID_W,) + bias.shape[2:], F32),
                        pltpu.VMEM((s // GRID_W,) + bias.shape[2:], BF16)],
        name="natten",
        compiler_params=_cparams("arbitrary", "arbitrary"),
    )(na, na, na, bias)


def _outproj_kernel(odn_ref, ona_ref, x_ref, wo_dn_ref, wo_na_ref, gpost_ref, gt1_ref, gpre_ref, sc_ref, sh_ref,
                    wr_ref, br_ref, x1_ref, h_ref, route_ref, cnt_ref, cnt_s):
    y = _dot(odn_ref[0], wo_dn_ref[...]) + _dot(ona_ref[0], wo_na_ref[...])
    x1 = x_ref[0] + gt1_ref[0] * (_rms(y) * gpost_ref[...])
    x1_ref[0] = x1
    h = _rms(x1) * gpre_ref[...]
    h = h * (1.0 + sc_ref[0]) + sh_ref[0]
    h_ref[0] = h

    logits = _dot(h, wr_ref[...], precision=HIGHEST) + br_ref[...]
    lane = lax.broadcasted_iota(jnp.int32, logits.shape, 1).astype(F32)
    big = float(LANES)

    def masked_softmax(mask):
        lm = jnp.where(mask, logits, NEG_BIG)
        e = jnp.where(mask, jnp.exp(lm - jnp.max(lm, axis=-1, keepdims=True)), 0.0)
        return e / jnp.sum(e, axis=-1, keepdims=True)

    def top1(p, mask):
        pm = jnp.where(mask, p, -1.0)
        best = jnp.max(pm, axis=-1, keepdims=True)
        idx = jnp.min(jnp.where(mask & (pm == best), lane, big), axis=-1, keepdims=True)
        return best, idx

    gmask = lane < N_GROUPS
    gp_top, g_idx = top1(masked_softmax(gmask), gmask)
    e_lo = N_GROUPS + g_idx * EXPERTS_PER_GROUP
    emask = (lane >= e_lo) & (lane < e_lo + EXPERTS_PER_GROUP)
    pe = masked_softmax(emask)
    p1, i1 = top1(pe, emask)
    p2, i2 = top1(pe, emask & (lane != i1))
    w1 = gp_top * p1 / (p1 + p2)
    w2 = gp_top * p2 / (p1 + p2)

    @pl.when((pl.program_id(0) == 0) & (pl.program_id(1) == 0))
    def _():
        cnt_s[...] = jnp.zeros_like(cnt_s)

    el = lane + N_GROUPS
    oh1 = el == i1
    oh2 = el == i2
    tm = logits.shape[0]
    ti, tj = _iota2((tm, tm))
    before = (ti > tj).astype(BF16)
    pre1 = _dot(before, oh1.astype(BF16))
    pre2 = _dot(before, oh2.astype(BF16))
    tot1 = jnp.sum(oh1.astype(F32), axis=0, keepdims=True)
    tot2 = jnp.sum(oh2.astype(F32), axis=0, keepdims=True)
    base = cnt_s[...]
    rank1 = jnp.sum(jnp.where(oh1, base + pre1, 0.0), axis=-1, keepdims=True)
    rank2 = jnp.sum(jnp.where(oh2, base + tot1 + pre2, 0.0), axis=-1, keepdims=True)
    cnt_s[...] = base + tot1 + tot2
    cnt_ref[...] = cnt_s[...]
    fields = (i1 - N_GROUPS, i2 - N_GROUPS, w1, w2, rank1, rank2)
    route = jnp.zeros_like(logits)
    for n, f in enumerate(fields):
        route = jnp.where(lane == n, f, route)
    route_ref[0] = route


def _outproj(odn, ona, x, wo_dn, wo_na, gpost, gt1, gpre, sc2, sh2, wr, br, tm=512):
    bsz, s, d = x.shape
    full = lambda a: pl.BlockSpec(a.shape, lambda b, i: (0,) * a.ndim)
    mod = pl.BlockSpec((1, 1, d), lambda b, i: (b, 0, 0))
    tok = lambda n: pl.BlockSpec((1, tm, n), lambda b, i: (b, i, 0))
    return pl.pallas_call(
        _outproj_kernel,
        grid=(bsz, s // tm),
        in_specs=[tok(odn.shape[2]), tok(ona.shape[2]), tok(d), full(wo_dn), full(wo_na), full(gpost), mod,
                  full(gpre), mod, mod, full(wr), full(br)],
        out_specs=[tok(d), tok(d), tok(LANES), pl.BlockSpec((1, LANES), lambda b, i: (0, 0))],
        out_shape=[jax.ShapeDtypeStruct((bsz, s, d), F32), jax.ShapeDtypeStruct((bsz, s, d), F32),
                   jax.ShapeDtypeStruct((bsz, s, LANES), F32), jax.ShapeDtypeStruct((1, LANES), F32)],
        scratch_shapes=[pltpu.VMEM((1, LANES), F32)],
        name="outproj_route",
        compiler_params=_cparams("arbitrary", "arbitrary"),
    )(odn, ona, x, wo_dn, wo_na, gpost, gt1, gpre, sc2, sh2, wr, br)


def _dispatch_kernel(pos_ref, h_ref, xs_in, xs_hbm, sem):
    del xs_in
    tb = pos_ref.shape[2]

    def issue(r, carry):
        src = h_ref.at[pl.ds(r, 1)]
        for slot in range(2):
            pltpu.make_async_copy(src, xs_hbm.at[pl.ds(pos_ref[0, slot, r], 1)], sem).start()
        return carry

    lax.fori_loop(0, tb, issue, 0, unroll=8)
    for _ in range(2):
        pltpu.make_async_copy(h_ref, xs_hbm.at[pl.ds(0, tb)], sem).wait()


def _dispatch(h, pos, n_rows, tb=512):
    t, d = h.shape
    return pl.pallas_call(
        _dispatch_kernel,
        grid=(t // tb,),
        in_specs=[pl.BlockSpec((1, 2, tb), lambda i: (i, 0, 0), memory_space=pltpu.SMEM),
                  pl.BlockSpec((tb, d), lambda i: (i, 0)),
                  pl.BlockSpec(memory_space=pl.ANY)],
        out_specs=pl.BlockSpec(memory_space=pl.ANY),
        out_shape=jax.ShapeDtypeStruct((n_rows, d), h.dtype),
        scratch_shapes=[pltpu.SemaphoreType.DMA(())],
        input_output_aliases={2: 0},
        name="moe_dispatch",
        compiler_params=_cparams("arbitrary"),
    )(pos, h, jnp.zeros((n_rows, d), h.dtype))


def _expert_kernel(te_ref, ts_ref, nu_ref, x_ref, wg_ref, wu_ref, wd_ref, y_ref, wg_s, wu_s, wd_s):
    j = pl.program_id(0)
    valid = j < nu_ref[0]
    new_expert = (j == 0) | (te_ref[j] != te_ref[jnp.maximum(j - 1, 0)])

    @pl.when(valid & new_expert)
    def _():
        wg_s[...] = wg_ref[0].astype(BF16)
        wu_s[...] = wu_ref[0].astype(BF16)
        wd_s[...] = wd_ref[0].astype(BF16)

    @pl.when(valid)
    def _():
        x = x_ref[...].astype(BF16)
        he = (_silu(_dot(x, wg_s[...])) * _dot(x, wu_s[...])).astype(BF16)
        y_ref[...] = _dot(he, wd_s[...])

    @pl.when(jnp.logical_not(valid))
    def _():
        y_ref[...] = jnp.zeros_like(y_ref)


def _experts(xs, tile_expert, tile_src, n_used, w_gate, w_up, w_down):
    n_rows, d = xs.shape
    n_exp, _, de = w_gate.shape
    n_tiles = n_rows // MOE_ROW_TILE
    rows = pl.BlockSpec((MOE_ROW_TILE, d), lambda j, te, ts, nu: (ts[j], 0))
    grid_spec = pltpu.PrefetchScalarGridSpec(
        num_scalar_prefetch=3,
        grid=(n_tiles,),
        in_specs=[rows,
                  pl.BlockSpec((1, d, de), lambda j, te, ts, nu: (te[j], 0, 0)),
                  pl.BlockSpec((1, d, de), lambda j, te, ts, nu: (te[j], 0, 0)),
                  pl.BlockSpec((1, de, d), lambda j, te, ts, nu: (te[j], 0, 0))],
        out_specs=pl.BlockSpec((MOE_ROW_TILE, d), lambda j, te, ts, nu: (j, 0)),
        scratch_shapes=[pltpu.VMEM((d, de), BF16), pltpu.VMEM((d, de), BF16), pltpu.VMEM((de, d), BF16)],
    )
    return pl.pallas_call(
        _expert_kernel,
        grid_spec=grid_spec,
        out_shape=jax.ShapeDtypeStruct((n_rows, d), F32),
        name="moe_experts",
        compiler_params=_cparams("arbitrary"),
    )(tile_expert, tile_src, n_used, xs, w_gate, w_up, w_down)


def _combine_kernel(pos_ref, posn_ref, route_ref, x1_ref, gpost_ref, gt2_ref, ys_hbm, o_ref, buf, sems):
    i = pl.program_id(0)
    n = pl.num_programs(0)
    tc = pos_ref.shape[2]

    def start_tile(p_ref, slot):
        def issue(r, carry):
            for k in range(2):
                pltpu.make_async_copy(ys_hbm.at[pl.ds(p_ref[0, k, r], 1)], buf.at[slot, k, pl.ds(r, 1)],
                                      sems.at[slot]).start()
            return carry
        lax.fori_loop(0, tc, issue, 0, unroll=8)

    @pl.when(i == 0)
    def _():
        start_tile(pos_ref, 0)

    @pl.when(i + 1 < n)
    def _():
        start_tile(posn_ref, (i + 1) % 2)

    slot = i % 2
    pltpu.make_async_copy(buf.at[slot], buf.at[slot], sems.at[slot]).wait()
    route = route_ref[...]
    lane = lax.broadcasted_iota(jnp.int32, route.shape, 1)
    w1 = jnp.sum(jnp.where(lane == 2, route, 0.0), axis=-1, keepdims=True)
    w2 = jnp.sum(jnp.where(lane == 3, route, 0.0), axis=-1, keepdims=True)
    y = w1 * buf[slot, 0] + w2 * buf[slot, 1]
    o_ref[...] = x1_ref[...] + gt2_ref[0] * (_rms(y) * gpost_ref[...])


def _combine(ys, pos, route, x1, gpost, gt2, seq, tc=256):
    t, d = x1.shape
    n = t // tc
    per_seq = seq // tc
    smem = lambda f: pl.BlockSpec((1, 2, tc), f, memory_space=pltpu.SMEM)
    return pl.pallas_call(
        _combine_kernel,
        grid=(n,),
        in_specs=[smem(lambda i: (i, 0, 0)), smem(lambda i: (jnp.minimum(i + 1, n - 1), 0, 0)),
                  pl.BlockSpec((tc, LANES), lambda i: (i, 0)),
                  pl.BlockSpec((tc, d), lambda i: (i, 0)),
                  pl.BlockSpec((1, d), lambda i: (0, 0)),
                  pl.BlockSpec((1, 1, d), lambda i: (i // per_seq, 0, 0)),
                  pl.BlockSpec(memory_space=pl.ANY)],
        out_specs=pl.BlockSpec((tc, d), lambda i: (i, 0)),
        out_shape=jax.ShapeDtypeStruct((t, d), F32),
        scratch_shapes=[pltpu.VMEM((2, 2, tc, d), F32), pltpu.SemaphoreType.DMA((2,))],
        name="moe_combine",
        compiler_params=_cparams("arbitrary"),
    )(pos, pos, route, x1, gpost, gt2, ys)


def _moe(h, route, counts, w_gate, w_up, w_down, x1, gpost, gt2):
    bsz, s, d = x1.shape
    t = bsz * s
    n_exp = w_gate.shape[0]
    tr = MOE_ROW_TILE
    n_tiles = 2 * t // tr + n_exp
    route = route.reshape(t, LANES)
    ids = route[:, 0:2].astype(jnp.int32)
    ranks = route[:, 4:6].astype(jnp.int32)
    cnt = counts[0, :n_exp].astype(jnp.int32)
    padded = (cnt + tr - 1) // tr * tr
    ends = jnp.cumsum(padded)
    offs = ends - padded
    onehot = ids[:, :, None] == jnp.arange(n_exp, dtype=jnp.int32)[None, None, :]
    pos = jnp.sum(jnp.where(onehot, offs[None, None, :], 0), axis=-1) + ranks
    n_used = ends[-1] // tr
    tile = jnp.arange(n_tiles, dtype=jnp.int32)
    tile_expert = jnp.sum((tile[:, None] >= (ends // tr)[None, :]).astype(jnp.int32), axis=-1)
    last = n_used - 1
    valid = tile < n_used
    tile_src = jnp.maximum(jnp.where(valid, tile, last), 0)
    tile_expert = jnp.minimum(jnp.where(valid, tile_expert, jnp.sum((last >= ends // tr).astype(jnp.int32))),
                              n_exp - 1)

    def tiles_of(tok_tile):
        return pos.reshape(t // tok_tile, tok_tile, 2).transpose(0, 2, 1)

    xs = _dispatch(h.reshape(t, d), tiles_of(512), n_tiles * tr)
    ys = _experts(xs, tile_expert, tile_src, n_used.reshape(1), w_gate, w_up, w_down)
    out = _combine(ys, tiles_of(256), route, x1.reshape(t, d), gpost, gt2, s)
    return out.reshape(bsz, s, d)


def _pad_lanes(a, n=LANES):
    return jnp.pad(a, [(0, 0)] * (a.ndim - 1) + [(0, n - a.shape[-1])])


def kernel(x, c, w_ada, b_ada, g_pre_mix, g_post_mix, w_in, w_conv_dn, a_log_dn, dt_bias_dn, g_onorm_dn, rpb_na,
           w_out, g_pre_ffn, g_post_ffn, w_group, b_group, w_expert, b_expert, w_gate, w_up, w_down):
    bsz, s, d = x.shape
    depth = w_ada.shape[0]
    n_dn = a_log_dn.shape[2]
    dn_w = n_dn * HEAD_DIM
    for l in range(depth):
        ada = _ada(c, w_ada[l], b_ada[l])
        sh1, sc1, gt1, sh2, sc2, gt2 = [a.reshape(bsz, 1, d) for a in jnp.split(ada, 6, axis=-1)]

        wi = w_in[l]
        o_z, o_ab, o_na = 3 * dn_w, 4 * dn_w, 4 * dn_w + 4 * n_dn
        wdn = wi[:, :o_z].astype(BF16)
        wz = wi[:, o_z:o_ab].astype(BF16)
        wab = _pad_lanes(wi[:, o_ab:o_na]).astype(BF16)
        na_w = (wi.shape[1] - o_na) // 3
        wna = jnp.concatenate([wi[:, o_na:o_na + na_w] * (HEAD_DIM ** -0.5), wi[:, o_na + na_w:]], axis=1).astype(BF16)
        alog = _pad_lanes(a_log_dn[l].reshape(1, 2 * n_dn))
        dtb = _pad_lanes(dt_bias_dn[l].reshape(1, 2 * n_dn))
        dn, z, gates, na = _inproj(x, sc1, sh1, g_pre_mix[l].reshape(1, d), wdn, wz, wab, wna, alog, dtb)

        gt = gates[:, :, :32].reshape(bsz, s // DN_CHUNK, DN_CHUNK, 32).transpose(0, 1, 3, 2)
        gon2 = jnp.tile(g_onorm_dn[l].reshape(1, HEAD_DIM), (1, LANES // HEAD_DIM))
        o_dn = _delta(dn, w_conv_dn[l], gates, gt, z, gon2)

        o_na = _natten(na, _na_bias_tables(rpb_na[l]))

        wo = w_out[l].astype(BF16)
        wr = _pad_lanes(jnp.concatenate([w_group[l], w_expert[l]], axis=1))
        br = _pad_lanes(jnp.concatenate([b_group[l], b_expert[l]]).reshape(1, -1))
        x1, h2, route, counts = _outproj(o_dn, o_na, x, wo[:dn_w], wo[dn_w:], g_post_mix[l].reshape(1, d), gt1,
                                g_pre_ffn[l].reshape(1, d), sc2, sh2, wr, br)
        x = _moe(h2, route, counts, w_gate[l], w_up[l], w_down[l], x1, g_post_ffn[l].reshape(1, d), gt2)
    return x
```

```python
import numpy as np
import jax
import jax.numpy as jnp
from jax import lax
from jax.experimental import pallas as pl
from jax.experimental.pallas import tpu as pltpu

F32 = jnp.float32
BF16 = jnp.bfloat16
HIGHEST = lax.Precision.HIGHEST

GRID_W = 64
HEAD_DIM = 64
DN_CONV = 5
DN_CHUNK = 128
NA_WIN_H = 8
NA_WIN_W = 16
N_GROUPS = 4
EXPERTS_PER_GROUP = 8
N_EXPERTS = N_GROUPS * EXPERTS_PER_GROUP
EPS = 1e-6
LANES = 128
NEG_BIG = -1e30
VMEM_LIMIT = 56 * 1024 * 1024
MOE_ROW_TILE = 256


def _cparams(*sem):
    return pltpu.CompilerParams(dimension_semantics=sem, vmem_limit_bytes=VMEM_LIMIT)


def _sigmoid(x):
    return 1.0 / (1.0 + jnp.exp(-x))


def _silu(x):
    return x * _sigmoid(x)


def _softplus(x):
    return jnp.maximum(x, 0.0) + jnp.log1p(jnp.exp(-jnp.abs(x)))


def _dot(a, b, **kw):
    return jnp.dot(a, b, preferred_element_type=F32, **kw)


def _dot_nt(a, b, **kw):
    return lax.dot_general(a, b, (((1,), (1,)), ((), ())), preferred_element_type=F32, **kw)


def _dot_tn(a, b, **kw):
    return lax.dot_general(a, b, (((0,), (0,)), ((), ())), preferred_element_type=F32, **kw)


def _rms(x):
    return x * lax.rsqrt(jnp.mean(x * x, axis=-1, keepdims=True) + EPS)


def _iota2(shape):
    return lax.broadcasted_iota(jnp.int32, shape, 0), lax.broadcasted_iota(jnp.int32, shape, 1)


def _ada_kernel(c_ref, w_ref, b_ref, o_ref):
    cond = _silu(c_ref[...])
    o_ref[...] = _dot(cond, w_ref[...], precision=HIGHEST) + b_ref[...]


def _ada(c, w_ada, b_ada):
    bsz, d = c.shape
    n = w_ada.shape[1]
    tn = 1536
    return pl.pallas_call(
        _ada_kernel,
        grid=(n // tn,),
        in_specs=[pl.BlockSpec((bsz, d), lambda j: (0, 0)),
                  pl.BlockSpec((d, tn), lambda j: (0, j)),
                  pl.BlockSpec((1, tn), lambda j: (0, j))],
        out_specs=pl.BlockSpec((bsz, tn), lambda j: (0, j)),
        out_shape=jax.ShapeDtypeStruct((bsz, n), F32),
        name="ada_ln",
        compiler_params=_cparams("arbitrary"),
    )(c, w_ada, b_ada.reshape(1, n))


def _inproj_kernel(x_ref, sc_ref, sh_ref, g_ref, wdn_ref, wz_ref, wab_ref, wna_ref, alog_ref, dtb_ref,
                   dn_ref, z_ref, gate_ref, na_ref):
    h = _rms(x_ref[0]) * g_ref[...]
    h = h * (1.0 + sc_ref[0]) + sh_ref[0]
    hb = h.astype(BF16)
    dn_ref[0] = _dot(hb, wdn_ref[...]).astype(BF16)
    z_ref[0] = _dot(hb, wz_ref[...]).astype(BF16)
    na_ref[0] = _dot(hb, wna_ref[...]).astype(BF16)
    ab = _dot(hb, wab_ref[...])
    lane = lax.broadcasted_iota(jnp.int32, ab.shape, 1)
    decay = -jnp.exp(alog_ref[...]) * _softplus(ab + dtb_ref[...])
    gate_ref[0] = jnp.where(lane < 8, _chunk_cumsum(decay, reverse=False),
                            jnp.where(lane < 16, _chunk_cumsum(decay, reverse=True),
                                      _sigmoid(ab)))


def _inproj(x, sc, sh, g, wdn, wz, wab, wna, alog, dtb, tm=512):
    bsz, s, d = x.shape
    ndn, nz, nna = wdn.shape[1], wz.shape[1], wna.shape[1]
    full = lambda a: pl.BlockSpec(a.shape, lambda b, i: (0,) * a.ndim)
    mod = pl.BlockSpec((1, 1, d), lambda b, i: (b, 0, 0))
    tok = lambda n: pl.BlockSpec((1, tm, n), lambda b, i: (b, i, 0))
    return pl.pallas_call(
        _inproj_kernel,
        grid=(bsz, s // tm),
        in_specs=[tok(d), mod, mod, full(g), full(wdn), full(wz), full(wab), full(wna), full(alog), full(dtb)],
        out_specs=[tok(ndn), tok(nz), tok(LANES), tok(nna)],
        out_shape=[jax.ShapeDtypeStruct((bsz, s, ndn), BF16), jax.ShapeDtypeStruct((bsz, s, nz), BF16),
                   jax.ShapeDtypeStruct((bsz, s, LANES), F32), jax.ShapeDtypeStruct((bsz, s, nna), BF16)],
        name="prenorm_inproj",
        compiler_params=_cparams("arbitrary", "arbitrary"),
    )(x, sc, sh, g, wdn, wz, wab, wna, alog, dtb)


def _conv_silu(x, w):
    s = x.shape[0]
    row = lax.broadcasted_iota(jnp.int32, x.shape, 0)
    pad = (DN_CONV - 1) // 2
    acc = x * w[pad:pad + 1]
    for j in range(DN_CONV):
        d = j - pad
        if d == 0:
            continue
        xs = pltpu.roll(x, (-d) % s, 0)
        ok = (row + d >= 0) & (row + d < s)
        acc = acc + jnp.where(ok, xs, 0.0) * w[j:j + 1]
    return _silu(acc)


def _chunk_cumsum(g, reverse):
    s = g.shape[0]
    pos = lax.broadcasted_iota(jnp.int32, g.shape, 0) & (DN_CHUNK - 1)
    sh = 1
    while sh < DN_CHUNK:
        if reverse:
            g = g + jnp.where(pos + sh < DN_CHUNK, pltpu.roll(g, s - sh, 0), 0.0)
        else:
            g = g + jnp.where(pos >= sh, pltpu.roll(g, sh, 0), 0.0)
        sh *= 2
    return g


def _delta_kernel(q_ref, k_ref, v_ref, wq_ref, wk_ref, wv_ref, gate_ref, gt_ref, z_ref, gon_ref, o_ref,
                  k_s, q_s, kb_s, kbg_s, qd_s, kt_s, vb_s, gci_s, egl_s, m_s, t0_s, t1_s, at_s, oacc_s):
    hp = pl.program_id(1)
    s = q_ref.shape[1]
    c = DN_CHUNK
    hd = HEAD_DIM
    n_chunks = s // c

    li, lj = _iota2((LANES, LANES))
    same_head_b = lax.shift_right_logical(li, 6) == lax.shift_right_logical(lj, 6)
    same_head = same_head_b.astype(F32)

    def l2n(x):
        return x * lax.rsqrt(_dot(x * x, same_head) + EPS)

    q2 = l2n(_conv_silu(q_ref[0].astype(F32), wq_ref[...])) * (hd ** -0.5)
    k2 = l2n(_conv_silu(k_ref[0].astype(F32), wk_ref[...]))
    v2 = _conv_silu(v_ref[0].astype(F32), wv_ref[...])
    k_s[...] = k2.astype(BF16)
    q_s[...] = q2.astype(BF16)

    gates = gate_ref[0]
    lane_s = lax.broadcasted_iota(jnp.int32, gates.shape, 1)

    def pair_cols(base):
        c0 = base + 2 * hp
        col0 = jnp.sum(jnp.where(lane_s == c0, gates, 0.0), axis=-1, keepdims=True)
        col1 = jnp.sum(jnp.where(lane_s == c0 + 1, gates, 0.0), axis=-1, keepdims=True)
        return jnp.where(lane_s < hd, col0, col1)

    ri, rj = _iota2((c, c))
    eye = (ri == rj).astype(F32)

    for direction in range(2):
        gci2 = pair_cols(8 * direction)
        beta2 = pair_cols(16 + 8 * direction)
        gc3 = gci2.reshape(n_chunks, c, LANES)
        tot = gc3[:, c - 1:c, :] if direction == 0 else gc3[:, 0:1, :]
        gl2 = jnp.broadcast_to(tot, gc3.shape).reshape(s, LANES)
        egc2 = jnp.exp(gci2)
        kb2 = k2 * beta2
        kb_s[direction] = kb2.astype(BF16)
        kbg_s[direction] = (kb2 * egc2).astype(BF16)
        qd_s[direction] = (q2 * egc2).astype(BF16)
        kt_s[direction] = (k2 * jnp.exp(gl2 - gci2)).astype(BF16)
        vb_s[direction] = v2 * beta2
        gci_s[direction] = gci2
        egl_s[direction] = jnp.exp(gl2)

    lane_c = lax.broadcasted_iota(jnp.int32, (c, LANES), 1)
    head0 = lane_c < hd

    same = lambda sh: lax.shift_right_logical(ri, sh) == lax.shift_right_logical(rj, sh)
    group = 16
    n_levels = c.bit_length() - 2
    t_bufs = (t0_s, t1_s)

    def chunk_rows(g, j):
        return pl.ds(pl.multiple_of((g * group + j) * c, c), c)

    def chunk_matrices(g, carry):
        for j in range(group):
            rows = chunk_rows(g, j)
            kc = k_s[rows, :]
            qc = q_s[rows, :]
            zero = jnp.zeros_like(kc)
            for direction in range(2):
                incl = (ri >= rj) if direction == 0 else (ri <= rj)
                strict = (ri > rj) if direction == 0 else (ri < rj)
                kbc = kb_s[direction, rows, :]
                gci = gci_s[direction, rows, :]
                for hh in range(2):
                    hm = head0 if hh == 0 else jnp.logical_not(head0)
                    gi = jnp.broadcast_to(gci[:, hh * hd:hh * hd + 1], (c, c))
                    gj = jnp.broadcast_to(
                        gt_ref[0, g * group + j, pl.ds(8 * direction + 2 * hp + hh, 1), :], (c, c))
                    decay = jnp.where(incl, jnp.exp(jnp.where(incl, gi - gj, 0.0)), 0.0)
                    m = jnp.where(strict, _dot_nt(jnp.where(hm, kbc, zero), kc) * decay, 0.0)
                    m_s[direction, hh, rows, :] = m.astype(BF16)
                    t_bufs[n_levels % 2][direction, hh, rows, :] =(eye - jnp.where(same(1), m, 0.0)).astype(BF16)
                    at_s[direction, hh, rows, :] = (_dot_nt(jnp.where(hm, qc, zero), kc) * decay).astype(BF16)
        return carry

    lax.fori_loop(0, n_chunks // group, chunk_matrices, 0)

    for level in range(n_levels):
        def merge_level(g, carry, level=level):
            sh = level + 1
            src = (n_levels - level) % 2
            cmask = same(sh + 1) & jnp.logical_not(same(sh))
            for j in range(group):
                rows = chunk_rows(g, j)
                for direction in range(2):
                    for hh in range(2):
                        tb = t_bufs[src][direction, hh, rows, :]
                        mb = m_s[direction, hh, rows, :]
                        cm = jnp.where(cmask, mb, jnp.zeros_like(mb))
                        y = _dot(tb, _dot(cm, tb).astype(BF16))
                        t_bufs[1 - src][direction, hh, rows, :] =jnp.where(cmask, (-y).astype(BF16), tb)
            return carry

        lax.fori_loop(0, n_chunks // group, merge_level, 0)

    oacc_s[...] = jnp.zeros_like(oacc_s)

    def scan_step(i, states):
        new_states = []
        for direction in range(2):
            n = i if direction == 0 else n_chunks - 1 - i
            r0 = pl.multiple_of(n * c, c)
            rows = pl.ds(r0, c)
            sb = states[direction].astype(BF16)
            rhs = (vb_s[direction, rows, :] - _dot(kbg_s[direction, rows, :], sb)).astype(BF16)
            v_new = jnp.where(head0, _dot(t0_s[direction, 0, rows, :], rhs),
                              _dot(t0_s[direction, 1, rows, :], rhs)).astype(BF16)
            intra = jnp.where(head0, _dot(at_s[direction, 0, rows, :], v_new),
                              _dot(at_s[direction, 1, rows, :], v_new))
            oacc_s[rows, :] += _dot(qd_s[direction, rows, :], sb) + intra
            upd = _dot_tn(kt_s[direction, rows, :], v_new)
            new_states.append(states[direction] * egl_s[direction, pl.ds(r0, 1), :]
                              + jnp.where(same_head_b, upd, 0.0))
        return tuple(new_states)

    zero_state = jnp.zeros((LANES, LANES), F32)
    lax.fori_loop(0, n_chunks, scan_step, (zero_state, zero_state))

    o2 = oacc_s[...]
    ms = _dot(o2 * o2, same_head) * (1.0 / hd)
    o2 = o2 * lax.rsqrt(ms + EPS) * gon_ref[...]
    o_ref[0] = (o2 * _silu(z_ref[0].astype(F32))).astype(BF16)


def _delta(dn, w_conv, gates, gt, z, gon2):
    bsz, s, _ = dn.shape
    n_pairs = z.shape[2] // LANES
    c = DN_CHUNK
    col = lambda off: pl.BlockSpec((1, s, LANES), lambda b, p: (b, 0, off + p))
    wcol = lambda off: pl.BlockSpec((DN_CONV, LANES), lambda b, p: (0, off + p))
    two = lambda dt: pltpu.VMEM((2, s, LANES), dt)
    return pl.pallas_call(
        _delta_kernel,
        grid=(bsz, n_pairs),
        in_specs=[col(0), col(n_pairs), col(2 * n_pairs), wcol(0), wcol(n_pairs), wcol(2 * n_pairs),
                  pl.BlockSpec((1, s, LANES), lambda b, p: (b, 0, 0)),
                  pl.BlockSpec((1, s // c, 32, c), lambda b, p: (b, 0, 0, 0)),
                  col(0),
                  pl.BlockSpec((1, LANES), lambda b, p: (0, 0))],
        out_specs=col(0),
        out_shape=jax.ShapeDtypeStruct((bsz, s, n_pairs * LANES), BF16),
        scratch_shapes=[pltpu.VMEM((s, LANES), BF16), pltpu.VMEM((s, LANES), BF16),
                        two(BF16), two(BF16), two(BF16), two(BF16),
                        two(F32), two(F32), two(F32),
                        pltpu.VMEM((2, 2, s, LANES), BF16), pltpu.VMEM((2, 2, s, LANES), BF16),
                        pltpu.VMEM((2, 2, s, LANES), BF16), pltpu.VMEM((2, 2, s, LANES), BF16),
                        pltpu.VMEM((s, LANES), F32)],
        name="delta_rule",
        compiler_params=_cparams("arbitrary", "arbitrary"),
    )(dn, dn, dn, w_conv, w_conv, w_conv, gates, gt, z, gon2)


def _na_bias_tables(rpb):
    n_heads = rpb.shape[0]
    cc = np.arange(GRID_W)
    c0 = np.clip(cc - NA_WIN_W // 2, 0, GRID_W - NA_WIN_W)
    kc = np.arange(GRID_W)
    in_win = (kc[None, :] >= c0[:, None]) & (kc[None, :] < c0[:, None] + NA_WIN_W)
    col_off = kc[None, :] - cc[:, None] + NA_WIN_W - 1
    onehot = (col_off[None] == np.arange(2 * NA_WIN_W - 1)[:, None, None]) & in_win[None]
    rows = jnp.stack([rpb[:, NA_WIN_H - 1 - d:2 * NA_WIN_H - 1 - d, :] for d in range(NA_WIN_H)], axis=1)
    rows = rows.reshape(n_heads // 2, 2, NA_WIN_H, NA_WIN_H, 2 * NA_WIN_W - 1).astype(F32)
    bias = jnp.einsum("phdic,cqk->pdhqik", rows, jnp.asarray(onehot, F32), precision=HIGHEST)
    bias = bias + jnp.asarray(np.where(in_win, 0.0, NEG_BIG), F32)[None, None, None, :, None, :]
    return bias.reshape(n_heads // 2, NA_WIN_H, 2 * GRID_W, NA_WIN_H * GRID_W)


def _natten_kernel(q_ref, k_ref, v_ref, bias_ref, o_ref, sc_s, p_s):
    s = q_ref.shape[1]
    rows = s // GRID_W
    w = GRID_W
    win = min(NA_WIN_H, rows)
    hd = HEAD_DIM
    lane = lax.broadcasted_iota(jnp.int32, (w, LANES), 1)
    first = lane < hd
    window = lambda r: min(max(r - win // 2, 0), rows - win)
    for r in range(rows):
        r0 = window(r)
        q = q_ref[0, r * w:(r + 1) * w, :]
        zero = jnp.zeros_like(q)
        q_st = jnp.concatenate([jnp.where(first, q, zero), jnp.where(first, zero, q)], axis=0)
        kw = k_ref[0, r0 * w:(r0 + win) * w, :]
        sc_s[r] = _dot_nt(q_st, kw) + bias_ref[0, r - r0]
    for r in range(rows):
        sc = sc_s[r]
        p = jnp.exp(sc - jnp.max(sc, axis=-1, keepdims=True))
        p_s[r] = (p * (1.0 / jnp.sum(p, axis=-1, keepdims=True))).astype(BF16)
    for r in range(rows):
        r0 = window(r)
        o = _dot(p_s[r], v_ref[0, r0 * w:(r0 + win) * w, :])
        o_ref[0, r * w:(r + 1) * w, :] = jnp.where(first, o[:w], o[w:]).astype(BF16)


def _natten(na, bias):
    bsz, s, n3 = na.shape
    n_pairs = n3 // (3 * LANES)
    col = lambda off: pl.BlockSpec((1, s, LANES), lambda p, b: (b, 0, off + p))
    return pl.pallas_call(
        _natten_kernel,
        grid=(n_pairs, bsz),
        in_specs=[col(0), col(n_pairs), col(2 * n_pairs),
                  pl.BlockSpec((1,) + bias.shape[1:], lambda p, b: (p, 0, 0, 0))],
        out_specs=col(0),
        out_shape=jax.ShapeDtypeStruct((bsz, s, n_pairs * LANES), BF16),
        scratch_shapes=[pltpu.VMEM((s // GRID_W,) + bias.shape[2:], F32),
                        pltpu.VMEM((s // GRID_W,) + bias.shape[2:], BF16)],
        name="natten",
        compiler_params=_cparams("arbitrary", "arbitrary"),
    )(na, na, na, bias)


def _outproj_kernel(odn_ref, ona_ref, x_ref, wo_dn_ref, wo_na_ref, gpost_ref, gt1_ref, gpre_ref, sc_ref, sh_ref,
                    wr_ref, br_ref, x1_ref, h_ref, route_ref, cnt_ref, cnt_s):
    y = _dot(odn_ref[0], wo_dn_ref[...]) + _dot(ona_ref[0], wo_na_ref[...])
    x1 = x_ref[0] + gt1_ref[0] * (_rms(y) * gpost_ref[...])
    x1_ref[0] = x1
    h = _rms(x1) * gpre_ref[...]
    h = h * (1.0 + sc_ref[0]) + sh_ref[0]
    h_ref[0] = h

    h_hi = h.astype(BF16)
    h_lo = (h - h_hi.astype(F32)).astype(BF16)
    logits = (_dot(h_hi, wr_ref[0]) + (_dot(h_lo, wr_ref[0]) + _dot(h_hi, wr_ref[1]))
              + br_ref[...])
    lane = lax.broadcasted_iota(jnp.int32, logits.shape, 1).astype(F32)
    big = float(LANES)

    def masked_softmax(mask):
        lm = jnp.where(mask, logits, NEG_BIG)
        e = jnp.where(mask, jnp.exp(lm - jnp.max(lm, axis=-1, keepdims=True)), 0.0)
        return e / jnp.sum(e, axis=-1, keepdims=True)

    def top1(p, mask):
        pm = jnp.where(mask, p, -1.0)
        best = jnp.max(pm, axis=-1, keepdims=True)
        idx = jnp.min(jnp.where(mask & (pm == best), lane, big), axis=-1, keepdims=True)
        return best, idx

    gmask = lane < N_GROUPS
    gp_top, g_idx = top1(masked_softmax(gmask), gmask)
    e_lo = N_GROUPS + g_idx * EXPERTS_PER_GROUP
    emask = (lane >= e_lo) & (lane < e_lo + EXPERTS_PER_GROUP)
    pe = masked_softmax(emask)
    p1, i1 = top1(pe, emask)
    p2, i2 = top1(pe, emask & (lane != i1))
    w1 = gp_top * p1 / (p1 + p2)
    w2 = gp_top * p2 / (p1 + p2)

    @pl.when((pl.program_id(0) == 0) & (pl.program_id(1) == 0))
    def _():
        cnt_s[...] = jnp.zeros_like(cnt_s)

    el = lane + N_GROUPS
    oh1 = el == i1
    oh2 = el == i2
    tm = logits.shape[0]
    ti, tj = _iota2((tm, tm))
    before = (ti > tj).astype(BF16)
    pre1 = _dot(before, oh1.astype(BF16))
    pre2 = _dot(before, oh2.astype(BF16))
    tot1 = jnp.sum(oh1.astype(F32), axis=0, keepdims=True)
    tot2 = jnp.sum(oh2.astype(F32), axis=0, keepdims=True)
    base = cnt_s[...]
    rank1 = jnp.sum(jnp.where(oh1, base + pre1, 0.0), axis=-1, keepdims=True)
    rank2 = jnp.sum(jnp.where(oh2, base + tot1 + pre2, 0.0), axis=-1, keepdims=True)
    cnt_s[...] = base + tot1 + tot2
    cnt_ref[...] = cnt_s[...]
    fields = (i1 - N_GROUPS, i2 - N_GROUPS, w1, w2, rank1, rank2)
    route = jnp.zeros_like(logits)
    for n, f in enumerate(fields):
        route = jnp.where(lane == n, f, route)
    route_ref[0] = route


def _outproj(odn, ona, x, wo_dn, wo_na, gpost, gt1, gpre, sc2, sh2, wr, br, tm=512):
    bsz, s, d = x.shape
    full = lambda a: pl.BlockSpec(a.shape, lambda b, i: (0,) * a.ndim)
    mod = pl.BlockSpec((1, 1, d), lambda b, i: (b, 0, 0))
    tok = lambda n: pl.BlockSpec((1, tm, n), lambda b, i: (b, i, 0))
    return pl.pallas_call(
        _outproj_kernel,
        grid=(bsz, s // tm),
        in_specs=[tok(odn.shape[2]), tok(ona.shape[2]), tok(d), full(wo_dn), full(wo_na), full(gpost), mod,
                  full(gpre), mod, mod, full(wr), full(br)],
        out_specs=[tok(d), tok(d), tok(LANES), pl.BlockSpec((1, LANES), lambda b, i: (0, 0))],
        out_shape=[jax.ShapeDtypeStruct((bsz, s, d), F32), jax.ShapeDtypeStruct((bsz, s, d), F32),
                   jax.ShapeDtypeStruct((bsz, s, LANES), F32), jax.ShapeDtypeStruct((1, LANES), F32)],
        scratch_shapes=[pltpu.VMEM((1, LANES), F32)],
        name="outproj_route",
        compiler_params=_cparams("arbitrary", "arbitrary"),
    )(odn, ona, x, wo_dn, wo_na, gpost, gt1, gpre, sc2, sh2, wr, br)


def _dispatch_kernel(pos_ref, h_ref, xs_in, xs_hbm, sem):
    del xs_in
    tb = pos_ref.shape[2]

    def issue(r, carry):
        src = h_ref.at[pl.ds(r, 1)]
        for slot in range(2):
            pltpu.make_async_copy(src, xs_hbm.at[pl.ds(pos_ref[0, slot, r], 1)], sem).start()
        return carry

    lax.fori_loop(0, tb, issue, 0, unroll=8)
    for _ in range(2):
        pltpu.make_async_copy(h_ref, xs_hbm.at[pl.ds(0, tb)], sem).wait()


def _dispatch(h, pos, n_rows, tb=512):
    t, d = h.shape
    return pl.pallas_call(
        _dispatch_kernel,
        grid=(t // tb,),
        in_specs=[pl.BlockSpec((1, 2, tb), lambda i: (i, 0, 0), memory_space=pltpu.SMEM),
                  pl.BlockSpec((tb, d), lambda i: (i, 0)),
                  pl.BlockSpec(memory_space=pl.ANY)],
        out_specs=pl.BlockSpec(memory_space=pl.ANY),
        out_shape=jax.ShapeDtypeStruct((n_rows, d), h.dtype),
        scratch_shapes=[pltpu.SemaphoreType.DMA(())],
        input_output_aliases={2: 0},
        name="moe_dispatch",
        compiler_params=_cparams("arbitrary"),
    )(pos, h, jnp.zeros((n_rows, d), h.dtype))


def _expert_kernel(te_ref, ts_ref, nu_ref, x_ref, wg_ref, wu_ref, wd_ref, y_ref, wg_s, wu_s, wd_s):
    j = pl.program_id(0)
    valid = j < nu_ref[0]
    new_expert = (j == 0) | (te_ref[j] != te_ref[jnp.maximum(j - 1, 0)])

    @pl.when(valid & new_expert)
    def _():
        wg_s[...] = wg_ref[0].astype(BF16)
        wu_s[...] = wu_ref[0].astype(BF16)
        wd_s[...] = wd_ref[0].astype(BF16)

    @pl.when(valid)
    def _():
        x = x_ref[...].astype(BF16)
        he = (_silu(_dot(x, wg_s[...])) * _dot(x, wu_s[...])).astype(BF16)
        y_ref[...] = _dot(he, wd_s[...])

    @pl.when(jnp.logical_not(valid))
    def _():
        y_ref[...] = jnp.zeros_like(y_ref)


def _experts(xs, tile_expert, tile_src, n_used, w_gate, w_up, w_down):
    n_rows, d = xs.shape
    n_exp, _, de = w_gate.shape
    n_tiles = n_rows // MOE_ROW_TILE
    rows = pl.BlockSpec((MOE_ROW_TILE, d), lambda j, te, ts, nu: (ts[j], 0))
    grid_spec = pltpu.PrefetchScalarGridSpec(
        num_scalar_prefetch=3,
        grid=(n_tiles,),
        in_specs=[rows,
                  pl.BlockSpec((1, d, de), lambda j, te, ts, nu: (te[j], 0, 0)),
                  pl.BlockSpec((1, d, de), lambda j, te, ts, nu: (te[j], 0, 0)),
                  pl.BlockSpec((1, de, d), lambda j, te, ts, nu: (te[j], 0, 0))],
        out_specs=pl.BlockSpec((MOE_ROW_TILE, d), lambda j, te, ts, nu: (j, 0)),
        scratch_shapes=[pltpu.VMEM((d, de), BF16), pltpu.VMEM((d, de), BF16), pltpu.VMEM((de, d), BF16)],
    )
    return pl.pallas_call(
        _expert_kernel,
        grid_spec=grid_spec,
        out_shape=jax.ShapeDtypeStruct((n_rows, d), F32),
        name="moe_experts",
        compiler_params=_cparams("arbitrary"),
    )(tile_expert, tile_src, n_used, xs, w_gate, w_up, w_down)


def _combine_kernel(pos_ref, posn_ref, route_ref, x1_ref, gpost_ref, gt2_ref, ys_hbm, o_ref, buf, sems):
    i = pl.program_id(0)
    n = pl.num_programs(0)
    tc = pos_ref.shape[2]

    def start_tile(p_ref, slot):
        def issue(r, carry):
            for k in range(2):
                pltpu.make_async_copy(ys_hbm.at[pl.ds(p_ref[0, k, r], 1)], buf.at[slot, k, pl.ds(r, 1)],
                                      sems.at[slot]).start()
            return carry
        lax.fori_loop(0, tc, issue, 0, unroll=8)

    @pl.when(i == 0)
    def _():
        start_tile(pos_ref, 0)

    @pl.when(i + 1 < n)
    def _():
        start_tile(posn_ref, (i + 1) % 2)

    slot = i % 2
    pltpu.make_async_copy(buf.at[slot], buf.at[slot], sems.at[slot]).wait()
    route = route_ref[...]
    lane = lax.broadcasted_iota(jnp.int32, route.shape, 1)
    w1 = jnp.sum(jnp.where(lane == 2, route, 0.0), axis=-1, keepdims=True)
    w2 = jnp.sum(jnp.where(lane == 3, route, 0.0), axis=-1, keepdims=True)
    y = w1 * buf[slot, 0] + w2 * buf[slot, 1]
    o_ref[...] = x1_ref[...] + gt2_ref[0] * (_rms(y) * gpost_ref[...])


def _combine(ys, pos, route, x1, gpost, gt2, seq, tc=256):
    t, d = x1.shape
    n = t // tc
    per_seq = seq // tc
    smem = lambda f: pl.BlockSpec((1, 2, tc), f, memory_space=pltpu.SMEM)
    return pl.pallas_call(
        _combine_kernel,
        grid=(n,),
        in_specs=[smem(lambda i: (i, 0, 0)), smem(lambda i: (jnp.minimum(i + 1, n - 1), 0, 0)),
                  pl.BlockSpec((tc, LANES), lambda i: (i, 0)),
                  pl.BlockSpec((tc, d), lambda i: (i, 0)),
                  pl.BlockSpec((1, d), lambda i: (0, 0)),
                  pl.BlockSpec((1, 1, d), lambda i: (i // per_seq, 0, 0)),
                  pl.BlockSpec(memory_space=pl.ANY)],
        out_specs=pl.BlockSpec((tc, d), lambda i: (i, 0)),
        out_shape=jax.ShapeDtypeStruct((t, d), F32),
        scratch_shapes=[pltpu.VMEM((2, 2, tc, d), F32), pltpu.SemaphoreType.DMA((2,))],
        name="moe_combine",
        compiler_params=_cparams("arbitrary"),
    )(pos, pos, route, x1, gpost, gt2, ys)


def _moe(h, route, counts, w_gate, w_up, w_down, x1, gpost, gt2):
    bsz, s, d = x1.shape
    t = bsz * s
    n_exp = w_gate.shape[0]
    tr = MOE_ROW_TILE
    n_tiles = 2 * t // tr + n_exp
    route = route.reshape(t, LANES)
    ids = route[:, 0:2].astype(jnp.int32)
    ranks = route[:, 4:6].astype(jnp.int32)
    cnt = counts[0, :n_exp].astype(jnp.int32)
    padded = (cnt + tr - 1) // tr * tr
    ends = jnp.cumsum(padded)
    offs = ends - padded
    onehot = ids[:, :, None] == jnp.arange(n_exp, dtype=jnp.int32)[None, None, :]
    pos = jnp.sum(jnp.where(onehot, offs[None, None, :], 0), axis=-1) + ranks
    n_used = ends[-1] // tr
    tile = jnp.arange(n_tiles, dtype=jnp.int32)
    tile_expert = jnp.sum((tile[:, None] >= (ends // tr)[None, :]).astype(jnp.int32), axis=-1)
    last = n_used - 1
    valid = tile < n_used
    tile_src = jnp.maximum(jnp.where(valid, tile, last), 0)
    tile_expert = jnp.minimum(jnp.where(valid, tile_expert, jnp.sum((last >= ends // tr).astype(jnp.int32))),
                              n_exp - 1)

    def tiles_of(tok_tile):
        return pos.reshape(t // tok_tile, tok_tile, 2).transpose(0, 2, 1)

    xs = _dispatch(h.reshape(t, d), tiles_of(512), n_tiles * tr)
    ys = _experts(xs, tile_expert, tile_src, n_used.reshape(1), w_gate, w_up, w_down)
    out = _combine(ys, tiles_of(256), route, x1.reshape(t, d), gpost, gt2, s)
    return out.reshape(bsz, s, d)


def _pad_lanes(a, n=LANES):
    return jnp.pad(a, [(0, 0)] * (a.ndim - 1) + [(0, n - a.shape[-1])])


def kernel(x, c, w_ada, b_ada, g_pre_mix, g_post_mix, w_in, w_conv_dn, a_log_dn, dt_bias_dn, g_onorm_dn, rpb_na,
           w_out, g_pre_ffn, g_post_ffn, w_group, b_group, w_expert, b_expert, w_gate, w_up, w_down):
    bsz, s, d = x.shape
    depth = w_ada.shape[0]
    n_dn = a_log_dn.shape[2]
    dn_w = n_dn * HEAD_DIM
    for l in range(depth):
        ada = _ada(c, w_ada[l], b_ada[l])
        sh1, sc1, gt1, sh2, sc2, gt2 = [a.reshape(bsz, 1, d) for a in jnp.split(ada, 6, axis=-1)]

        wi = w_in[l]
        o_z, o_ab, o_na = 3 * dn_w, 4 * dn_w, 4 * dn_w + 4 * n_dn
        wdn = wi[:, :o_z].astype(BF16)
        wz = wi[:, o_z:o_ab].astype(BF16)
        wab = _pad_lanes(wi[:, o_ab:o_na]).astype(BF16)
        na_w = (wi.shape[1] - o_na) // 3
        wna = jnp.concatenate([wi[:, o_na:o_na + na_w] * (HEAD_DIM ** -0.5), wi[:, o_na + na_w:]], axis=1).astype(BF16)
        alog = _pad_lanes(a_log_dn[l].reshape(1, 2 * n_dn))
        dtb = _pad_lanes(dt_bias_dn[l].reshape(1, 2 * n_dn))
        dn, z, gates, na = _inproj(x, sc1, sh1, g_pre_mix[l].reshape(1, d), wdn, wz, wab, wna, alog, dtb)

        gt = gates[:, :, :32].reshape(bsz, s // DN_CHUNK, DN_CHUNK, 32).transpose(0, 1, 3, 2)
        gon2 = jnp.tile(g_onorm_dn[l].reshape(1, HEAD_DIM), (1, LANES // HEAD_DIM))
        o_dn = _delta(dn, w_conv_dn[l], gates, gt, z, gon2)

        o_na = _natten(na, _na_bias_tables(rpb_na[l]))

        wo = w_out[l].astype(BF16)
        wr = _pad_lanes(jnp.concatenate([w_group[l], w_expert[l]], axis=1))
        wr_hi = wr.astype(BF16)
        wr = jnp.stack([wr_hi, (wr - wr_hi.astype(F32)).astype(BF16)])
        br = _pad_lanes(jnp.concatenate([b_group[l], b_expert[l]]).reshape(1, -1))
        x1, h2, route, counts = _outproj(o_dn, o_na, x, wo[:dn_w], wo[dn_w:], g_post_mix[l].reshape(1, d), gt1,
                                g_pre_ffn[l].reshape(1, d), sc2, sh2, wr, br)
        x = _moe(h2, route, counts, w_gate[l], w_up[l], w_down[l], x1, g_post_ffn[l].reshape(1, d), gt2)
    return x
```

```python
import numpy as np
import jax
import jax.numpy as jnp
from jax import lax
from jax.experimental import pallas as pl
from jax.experimental.pallas import tpu as pltpu

F32 = jnp.float32
BF16 = jnp.bfloat16
HIGHEST = lax.Precision.HIGHEST

GRID_W = 64
HEAD_DIM = 64
DN_CONV = 5
DN_CHUNK = 128
NA_WIN_H = 8
NA_WIN_W = 16
N_GROUPS = 4
EXPERTS_PER_GROUP = 8
N_EXPERTS = N_GROUPS * EXPERTS_PER_GROUP
EPS = 1e-6
LANES = 128
NEG_BIG = -1e30
VMEM_LIMIT = 56 * 1024 * 1024
MOE_ROW_TILE = 256


def _cparams(*sem):
    return pltpu.CompilerParams(dimension_semantics=sem, vmem_limit_bytes=VMEM_LIMIT)


def _sigmoid(x):
    return 1.0 / (1.0 + jnp.exp(-x))


def _silu(x):
    return x * _sigmoid(x)


def _softplus(x):
    return jnp.maximum(x, 0.0) + jnp.log1p(jnp.exp(-jnp.abs(x)))


def _dot(a, b, **kw):
    return jnp.dot(a, b, preferred_element_type=F32, **kw)


def _dot_nt(a, b, **kw):
    return lax.dot_general(a, b, (((1,), (1,)), ((), ())), preferred_element_type=F32, **kw)


def _dot_tn(a, b, **kw):
    return lax.dot_general(a, b, (((0,), (0,)), ((), ())), preferred_element_type=F32, **kw)


def _rms(x):
    return x * lax.rsqrt(jnp.mean(x * x, axis=-1, keepdims=True) + EPS)


def _iota2(shape):
    return lax.broadcasted_iota(jnp.int32, shape, 0), lax.broadcasted_iota(jnp.int32, shape, 1)


def _ada_kernel(c_ref, w_ref, b_ref, o_ref):
    cond = _silu(c_ref[...])
    o_ref[...] = _dot(cond, w_ref[...], precision=HIGHEST) + b_ref[...]


def _ada(c, w_ada, b_ada):
    bsz, d = c.shape
    n = w_ada.shape[1]
    tn = 1536
    return pl.pallas_call(
        _ada_kernel,
        grid=(n // tn,),
        in_specs=[pl.BlockSpec((bsz, d), lambda j: (0, 0)),
                  pl.BlockSpec((d, tn), lambda j: (0, j)),
                  pl.BlockSpec((1, tn), lambda j: (0, j))],
        out_specs=pl.BlockSpec((bsz, tn), lambda j: (0, j)),
        out_shape=jax.ShapeDtypeStruct((bsz, n), F32),
        name="ada_ln",
        compiler_params=_cparams("arbitrary"),
    )(c, w_ada, b_ada.reshape(1, n))


def _inproj_kernel(x_ref, sc_ref, sh_ref, g_ref, wdn_ref, wz_ref, wab_ref, wna_ref, alog_ref, dtb_ref,
                   dn_ref, z_ref, gate_ref, na_ref):
    h = _rms(x_ref[0]) * g_ref[...]
    h = h * (1.0 + sc_ref[0]) + sh_ref[0]
    hb = h.astype(BF16)
    dn_ref[0] = _dot(hb, wdn_ref[...]).astype(BF16)
    z_ref[0] = _dot(hb, wz_ref[...]).astype(BF16)
    na_ref[0] = _dot(hb, wna_ref[...]).astype(BF16)
    ab = _dot(hb, wab_ref[...])
    lane = lax.broadcasted_iota(jnp.int32, ab.shape, 1)
    decay = -jnp.exp(alog_ref[...]) * _softplus(ab + dtb_ref[...])
    gate_ref[0] = jnp.where(lane < 8, _chunk_cumsum(decay, reverse=False),
                            jnp.where(lane < 16, _chunk_cumsum(decay, reverse=True),
                                      _sigmoid(ab)))


def _inproj(x, sc, sh, g, wdn, wz, wab, wna, alog, dtb, tm=512):
    bsz, s, d = x.shape
    ndn, nz, nna = wdn.shape[1], wz.shape[1], wna.shape[1]
    full = lambda a: pl.BlockSpec(a.shape, lambda b, i: (0,) * a.ndim)
    mod = pl.BlockSpec((1, 1, d), lambda b, i: (b, 0, 0))
    tok = lambda n: pl.BlockSpec((1, tm, n), lambda b, i: (b, i, 0))
    return pl.pallas_call(
        _inproj_kernel,
        grid=(bsz, s // tm),
        in_specs=[tok(d), mod, mod, full(g), full(wdn), full(wz), full(wab), full(wna), full(alog), full(dtb)],
        out_specs=[tok(ndn), tok(nz), tok(LANES), tok(nna)],
        out_shape=[jax.ShapeDtypeStruct((bsz, s, ndn), BF16), jax.ShapeDtypeStruct((bsz, s, nz), BF16),
                   jax.ShapeDtypeStruct((bsz, s, LANES), F32), jax.ShapeDtypeStruct((bsz, s, nna), BF16)],
        name="prenorm_inproj",
        compiler_params=_cparams("arbitrary", "arbitrary"),
    )(x, sc, sh, g, wdn, wz, wab, wna, alog, dtb)


def _conv_silu(x, w):
    s = x.shape[0]
    row = lax.broadcasted_iota(jnp.int32, x.shape, 0)
    pad = (DN_CONV - 1) // 2
    acc = x * w[pad:pad + 1]
    for j in range(DN_CONV):
        d = j - pad
        if d == 0:
            continue
        xs = pltpu.roll(x, (-d) % s, 0)
        ok = (row + d >= 0) & (row + d < s)
        acc = acc + jnp.where(ok, xs, 0.0) * w[j:j + 1]
    return _silu(acc)


def _chunk_cumsum(g, reverse):
    s = g.shape[0]
    pos = lax.broadcasted_iota(jnp.int32, g.shape, 0) & (DN_CHUNK - 1)
    sh = 1
    while sh < DN_CHUNK:
        if reverse:
            g = g + jnp.where(pos + sh < DN_CHUNK, pltpu.roll(g, s - sh, 0), 0.0)
        else:
            g = g + jnp.where(pos >= sh, pltpu.roll(g, sh, 0), 0.0)
        sh *= 2
    return g


def _delta_kernel(q_ref, k_ref, v_ref, wq_ref, wk_ref, wv_ref, gate_ref, gt_ref, z_ref, gon_ref, o_ref,
                  k_s, q_s, kb_s, kbg_s, qd_s, kt_s, vb_s, gci_s, egl_s, m_s, t0_s, t1_s, at_s, oacc_s, w_s, u_s):
    hp = pl.program_id(1)
    s = q_ref.shape[1]
    c = DN_CHUNK
    hd = HEAD_DIM
    n_chunks = s // c

    li, lj = _iota2((LANES, LANES))
    same_head_b = lax.shift_right_logical(li, 6) == lax.shift_right_logical(lj, 6)
    same_head = same_head_b.astype(F32)

    def l2n(x):
        return x * lax.rsqrt(_dot(x * x, same_head) + EPS)

    q2 = l2n(_conv_silu(q_ref[0].astype(F32), wq_ref[...])) * (hd ** -0.5)
    k2 = l2n(_conv_silu(k_ref[0].astype(F32), wk_ref[...]))
    v2 = _conv_silu(v_ref[0].astype(F32), wv_ref[...])
    k_s[...] = k2.astype(BF16)
    q_s[...] = q2.astype(BF16)

    gates = gate_ref[0]
    lane_s = lax.broadcasted_iota(jnp.int32, gates.shape, 1)

    def pair_cols(base):
        c0 = base + 2 * hp
        col0 = jnp.sum(jnp.where(lane_s == c0, gates, 0.0), axis=-1, keepdims=True)
        col1 = jnp.sum(jnp.where(lane_s == c0 + 1, gates, 0.0), axis=-1, keepdims=True)
        return jnp.where(lane_s < hd, col0, col1)

    ri, rj = _iota2((c, c))
    eye = (ri == rj).astype(F32)

    for direction in range(2):
        gci2 = pair_cols(8 * direction)
        beta2 = pair_cols(16 + 8 * direction)
        gc3 = gci2.reshape(n_chunks, c, LANES)
        tot = gc3[:, c - 1:c, :] if direction == 0 else gc3[:, 0:1, :]
        gl2 = jnp.broadcast_to(tot, gc3.shape).reshape(s, LANES)
        egc2 = jnp.exp(gci2)
        kb2 = k2 * beta2
        kb_s[direction] = kb2.astype(BF16)
        kbg_s[direction] = (kb2 * egc2).astype(BF16)
        qd_s[direction] = (q2 * egc2).astype(BF16)
        kt_s[direction] = (k2 * jnp.exp(gl2 - gci2)).astype(BF16)
        vb_s[direction] = v2 * beta2
        gci_s[direction] = gci2
        egl_s[direction] = jnp.exp(gl2)

    lane_c = lax.broadcasted_iota(jnp.int32, (c, LANES), 1)
    head0 = lane_c < hd

    same = lambda sh: lax.shift_right_logical(ri, sh) == lax.shift_right_logical(rj, sh)
    group = 16
    n_levels = c.bit_length() - 2
    t_bufs = (t0_s, t1_s)

    def chunk_rows(g, j):
        return pl.ds(pl.multiple_of((g * group + j) * c, c), c)

    def chunk_matrices(g, carry):
        for j in range(group):
            rows = chunk_rows(g, j)
            kc = k_s[rows, :]
            qc = q_s[rows, :]
            zero = jnp.zeros_like(kc)
            for direction in range(2):
                incl = (ri >= rj) if direction == 0 else (ri <= rj)
                strict = (ri > rj) if direction == 0 else (ri < rj)
                kbc = kb_s[direction, rows, :]
                gci = gci_s[direction, rows, :]
                for hh in range(2):
                    hm = head0 if hh == 0 else jnp.logical_not(head0)
                    gi = jnp.broadcast_to(gci[:, hh * hd:hh * hd + 1], (c, c))
                    gj = jnp.broadcast_to(
                        gt_ref[0, g * group + j, pl.ds(8 * direction + 2 * hp + hh, 1), :], (c, c))
                    decay = jnp.where(incl, jnp.exp(jnp.where(incl, gi - gj, 0.0)), 0.0)
                    m = jnp.where(strict, _dot_nt(jnp.where(hm, kbc, zero), kc) * decay, 0.0)
                    m_s[direction, hh, rows, :] = m.astype(BF16)
                    t_bufs[n_levels % 2][direction, hh, rows, :] =(eye - jnp.where(same(1), m, 0.0)).astype(BF16)
                    at_s[direction, hh, rows, :] = (_dot_nt(jnp.where(hm, qc, zero), kc) * decay).astype(BF16)
        return carry

    lax.fori_loop(0, n_chunks // group, chunk_matrices, 0)

    for level in range(n_levels):
        def merge_level(g, carry, level=level):
            sh = level + 1
            src = (n_levels - level) % 2
            cmask = same(sh + 1) & jnp.logical_not(same(sh))
            for j in range(group):
                rows = chunk_rows(g, j)
                for direction in range(2):
                    for hh in range(2):
                        tb = t_bufs[src][direction, hh, rows, :]
                        mb = m_s[direction, hh, rows, :]
                        cm = jnp.where(cmask, mb, jnp.zeros_like(mb))
                        y = _dot(tb, _dot(cm, tb).astype(BF16))
                        t_bufs[1 - src][direction, hh, rows, :] =jnp.where(cmask, (-y).astype(BF16), tb)
            return carry

        lax.fori_loop(0, n_chunks // group, merge_level, 0)

    def apply_inverse(g, carry):
        for j in range(group):
            rows = chunk_rows(g, j)
            for direction in range(2):
                kbg = kbg_s[direction, rows, :]
                vb = vb_s[direction, rows, :].astype(BF16)
                th0 = t0_s[direction, 0, rows, :]
                th1 = t0_s[direction, 1, rows, :]
                w_s[direction, rows, :] = jnp.where(head0, _dot(th0, kbg), _dot(th1, kbg)).astype(BF16)
                u_s[direction, rows, :] = jnp.where(head0, _dot(th0, vb), _dot(th1, vb))
        return carry

    lax.fori_loop(0, n_chunks // group, apply_inverse, 0)

    oacc_s[...] = jnp.zeros_like(oacc_s)

    def scan_step(i, states):
        new_states = []
        for direction in range(2):
            n = i if direction == 0 else n_chunks - 1 - i
            r0 = pl.multiple_of(n * c, c)
            rows = pl.ds(r0, c)
            sb = states[direction].astype(BF16)
            v_new = (u_s[direction, rows, :] - _dot(w_s[direction, rows, :], sb)).astype(BF16)
            intra = jnp.where(head0, _dot(at_s[direction, 0, rows, :], v_new),
                              _dot(at_s[direction, 1, rows, :], v_new))
            oacc_s[rows, :] += _dot(qd_s[direction, rows, :], sb) + intra
            upd = _dot_tn(kt_s[direction, rows, :], v_new)
            new_states.append(states[direction] * egl_s[direction, pl.ds(r0, 1), :]
                              + jnp.where(same_head_b, upd, 0.0))
        return tuple(new_states)

    zero_state = jnp.zeros((LANES, LANES), F32)
    lax.fori_loop(0, n_chunks, scan_step, (zero_state, zero_state), unroll=True)

    o2 = oacc_s[...]
    ms = _dot(o2 * o2, same_head) * (1.0 / hd)
    o2 = o2 * lax.rsqrt(ms + EPS) * gon_ref[...]
    o_ref[0] = (o2 * _silu(z_ref[0].astype(F32))).astype(BF16)


def _delta(dn, w_conv, gates, gt, z, gon2):
    bsz, s, _ = dn.shape
    n_pairs = z.shape[2] // LANES
    c = DN_CHUNK
    col = lambda off: pl.BlockSpec((1, s, LANES), lambda b, p: (b, 0, off + p))
    wcol = lambda off: pl.BlockSpec((DN_CONV, LANES), lambda b, p: (0, off + p))
    two = lambda dt: pltpu.VMEM((2, s, LANES), dt)
    return pl.pallas_call(
        _delta_kernel,
        grid=(bsz, n_pairs),
        in_specs=[col(0), col(n_pairs), col(2 * n_pairs), wcol(0), wcol(n_pairs), wcol(2 * n_pairs),
                  pl.BlockSpec((1, s, LANES), lambda b, p: (b, 0, 0)),
                  pl.BlockSpec((1, s // c, 32, c), lambda b, p: (b, 0, 0, 0)),
                  col(0),
                  pl.BlockSpec((1, LANES), lambda b, p: (0, 0))],
        out_specs=col(0),
        out_shape=jax.ShapeDtypeStruct((bsz, s, n_pairs * LANES), BF16),
        scratch_shapes=[pltpu.VMEM((s, LANES), BF16), pltpu.VMEM((s, LANES), BF16),
                        two(BF16), two(BF16), two(BF16), two(BF16),
                        two(F32), two(F32), two(F32),
                        pltpu.VMEM((2, 2, s, LANES), BF16), pltpu.VMEM((2, 2, s, LANES), BF16),
                        pltpu.VMEM((2, 2, s, LANES), BF16), pltpu.VMEM((2, 2, s, LANES), BF16),
                        pltpu.VMEM((s, LANES), F32), two(BF16), two(F32)],
        name="delta_rule",
        compiler_params=_cparams("arbitrary", "arbitrary"),
    )(dn, dn, dn, w_conv, w_conv, w_conv, gates, gt, z, gon2)


def _na_bias_tables(rpb):
    n_heads = rpb.shape[0]
    cc = np.arange(GRID_W)
    c0 = np.clip(cc - NA_WIN_W // 2, 0, GRID_W - NA_WIN_W)
    kc = np.arange(GRID_W)
    in_win = (kc[None, :] >= c0[:, None]) & (kc[None, :] < c0[:, None] + NA_WIN_W)
    col_off = kc[None, :] - cc[:, None] + NA_WIN_W - 1
    onehot = (col_off[None] == np.arange(2 * NA_WIN_W - 1)[:, None, None]) & in_win[None]
    rows = jnp.stack([rpb[:, NA_WIN_H - 1 - d:2 * NA_WIN_H - 1 - d, :] for d in range(NA_WIN_H)], axis=1)
    rows = rows.reshape(n_heads // 2, 2, NA_WIN_H, NA_WIN_H, 2 * NA_WIN_W - 1).astype(F32)
    bias = jnp.einsum("phdic,cqk->pdhqik", rows, jnp.asarray(onehot, F32), precision=HIGHEST)
    bias = bias + jnp.asarray(np.where(in_win, 0.0, NEG_BIG), F32)[None, None, None, :, None, :]
    return bias.reshape(n_heads // 2, NA_WIN_H, 2 * GRID_W, NA_WIN_H * GRID_W)


def _natten_kernel(q_ref, k_ref, v_ref, bias_ref, o_ref, sc_s, p_s):
    s = q_ref.shape[1]
    rows = s // GRID_W
    w = GRID_W
    win = min(NA_WIN_H, rows)
    hd = HEAD_DIM
    lane = lax.broadcasted_iota(jnp.int32, (w, LANES), 1)
    first = lane < hd
    window = lambda r: min(max(r - win // 2, 0), rows - win)
    for r in range(rows):
        r0 = window(r)
        q = q_ref[0, r * w:(r + 1) * w, :]
        zero = jnp.zeros_like(q)
        q_st = jnp.concatenate([jnp.where(first, q, zero), jnp.where(first, zero, q)], axis=0)
        kw = k_ref[0, r0 * w:(r0 + win) * w, :]
        sc_s[r] = _dot_nt(q_st, kw) + bias_ref[0, r - r0]
    for r in range(rows):
        sc = sc_s[r]
        p = jnp.exp(sc - jnp.max(sc, axis=-1, keepdims=True))
        p_s[r] = (p * (1.0 / jnp.sum(p, axis=-1, keepdims=True))).astype(BF16)
    for r in range(rows):
        r0 = window(r)
        o = _dot(p_s[r], v_ref[0, r0 * w:(r0 + win) * w, :])
        o_ref[0, r * w:(r + 1) * w, :] = jnp.where(first, o[:w], o[w:]).astype(BF16)


def _natten(na, bias):
    bsz, s, n3 = na.shape
    n_pairs = n3 // (3 * LANES)
    col = lambda off: pl.BlockSpec((1, s, LANES), lambda p, b: (b, 0, off + p))
    return pl.pallas_call(
        _natten_kernel,
        grid=(n_pairs, bsz),
        in_specs=[col(0), col(n_pairs), col(2 * n_pairs),
                  pl.BlockSpec((1,) + bias.shape[1:], lambda p, b: (p, 0, 0, 0))],
        out_specs=col(0),
        out_shape=jax.ShapeDtypeStruct((bsz, s, n_pairs * LANES), BF16),
        scratch_shapes=[pltpu.VMEM((s // GRID_W,) + bias.shape[2:], F32),
                        pltpu.VMEM((s // GRID_W,) + bias.shape[2:], BF16)],
        name="natten",
        compiler_params=_cparams("arbitrary", "arbitrary"),
    )(na, na, na, bias)


def _outproj_kernel(odn_ref, ona_ref, x_ref, wo_dn_ref, wo_na_ref, gpost_ref, gt1_ref, gpre_ref, sc_ref, sh_ref,
                    wr_ref, br_ref, x1_ref, h_ref, route_ref, cnt_ref, cnt_s):
    y = _dot(odn_ref[0], wo_dn_ref[...]) + _dot(ona_ref[0], wo_na_ref[...])
    x1 = x_ref[0] + gt1_ref[0] * (_rms(y) * gpost_ref[...])
    x1_ref[0] = x1
    h = _rms(x1) * gpre_ref[...]
    h = h * (1.0 + sc_ref[0]) + sh_ref[0]
    h_ref[0] = h

    h_hi = h.astype(BF16)
    h_lo = (h - h_hi.astype(F32)).astype(BF16)
    logits = (_dot(h_hi, wr_ref[0]) + (_dot(h_lo, wr_ref[0]) + _dot(h_hi, wr_ref[1]))
              + br_ref[...])
    lane = lax.broadcasted_iota(jnp.int32, logits.shape, 1).astype(F32)
    big = float(LANES)

    def masked_softmax(mask):
        lm = jnp.where(mask, logits, NEG_BIG)
        e = jnp.where(mask, jnp.exp(lm - jnp.max(lm, axis=-1, keepdims=True)), 0.0)
        return e / jnp.sum(e, axis=-1, keepdims=True)

    def top1(p, mask):
        pm = jnp.where(mask, p, -1.0)
        best = jnp.max(pm, axis=-1, keepdims=True)
        idx = jnp.min(jnp.where(mask & (pm == best), lane, big), axis=-1, keepdims=True)
        return best, idx

    gmask = lane < N_GROUPS
    gp_top, g_idx = top1(masked_softmax(gmask), gmask)
    e_lo = N_GROUPS + g_idx * EXPERTS_PER_GROUP
    emask = (lane >= e_lo) & (lane < e_lo + EXPERTS_PER_GROUP)
    pe = masked_softmax(emask)
    p1, i1 = top1(pe, emask)
    p2, i2 = top1(pe, emask & (lane != i1))
    w1 = gp_top * p1 / (p1 + p2)
    w2 = gp_top * p2 / (p1 + p2)

    @pl.when((pl.program_id(0) == 0) & (pl.program_id(1) == 0))
    def _():
        cnt_s[...] = jnp.zeros_like(cnt_s)

    el = lane + N_GROUPS
    oh1 = el == i1
    oh2 = el == i2
    tm = logits.shape[0]
    ti, tj = _iota2((tm, tm))
    before = (ti > tj).astype(BF16)
    pre1 = _dot(before, oh1.astype(BF16))
    pre2 = _dot(before, oh2.astype(BF16))
    tot1 = jnp.sum(oh1.astype(F32), axis=0, keepdims=True)
    tot2 = jnp.sum(oh2.astype(F32), axis=0, keepdims=True)
    base = cnt_s[...]
    rank1 = jnp.sum(jnp.where(oh1, base + pre1, 0.0), axis=-1, keepdims=True)
    rank2 = jnp.sum(jnp.where(oh2, base + tot1 + pre2, 0.0), axis=-1, keepdims=True)
    cnt_s[...] = base + tot1 + tot2
    cnt_ref[...] = cnt_s[...]
    fields = (i1 - N_GROUPS, i2 - N_GROUPS, w1, w2, rank1, rank2)
    route = jnp.zeros_like(logits)
    for n, f in enumerate(fields):
        route = jnp.where(lane == n, f, route)
    route_ref[0] = route


def _outproj(odn, ona, x, wo_dn, wo_na, gpost, gt1, gpre, sc2, sh2, wr, br, tm=512):
    bsz, s, d = x.shape
    full = lambda a: pl.BlockSpec(a.shape, lambda b, i: (0,) * a.ndim)
    mod = pl.BlockSpec((1, 1, d), lambda b, i: (b, 0, 0))
    tok = lambda n: pl.BlockSpec((1, tm, n), lambda b, i: (b, i, 0))
    return pl.pallas_call(
        _outproj_kernel,
        grid=(bsz, s // tm),
        in_specs=[tok(odn.shape[2]), tok(ona.shape[2]), tok(d), full(wo_dn), full(wo_na), full(gpost), mod,
                  full(gpre), mod, mod, full(wr), full(br)],
        out_specs=[tok(d), tok(d), tok(LANES), pl.BlockSpec((1, LANES), lambda b, i: (0, 0))],
        out_shape=[jax.ShapeDtypeStruct((bsz, s, d), F32), jax.ShapeDtypeStruct((bsz, s, d), F32),
                   jax.ShapeDtypeStruct((bsz, s, LANES), F32), jax.ShapeDtypeStruct((1, LANES), F32)],
        scratch_shapes=[pltpu.VMEM((1, LANES), F32)],
        name="outproj_route",
        compiler_params=_cparams("arbitrary", "arbitrary"),
    )(odn, ona, x, wo_dn, wo_na, gpost, gt1, gpre, sc2, sh2, wr, br)


def _dispatch_kernel(pos_ref, h_ref, xs_in, xs_hbm, sem):
    del xs_in
    tb = pos_ref.shape[2]

    def issue(r, carry):
        src = h_ref.at[pl.ds(r, 1)]
        for slot in range(2):
            pltpu.make_async_copy(src, xs_hbm.at[pl.ds(pos_ref[0, slot, r], 1)], sem).start()
        return carry

    lax.fori_loop(0, tb, issue, 0, unroll=8)
    for _ in range(2):
        pltpu.make_async_copy(h_ref, xs_hbm.at[pl.ds(0, tb)], sem).wait()


def _dispatch(h, pos, n_rows, tb=512):
    t, d = h.shape
    return pl.pallas_call(
        _dispatch_kernel,
        grid=(t // tb,),
        in_specs=[pl.BlockSpec((1, 2, tb), lambda i: (i, 0, 0), memory_space=pltpu.SMEM),
                  pl.BlockSpec((tb, d), lambda i: (i, 0)),
                  pl.BlockSpec(memory_space=pl.ANY)],
        out_specs=pl.BlockSpec(memory_space=pl.ANY),
        out_shape=jax.ShapeDtypeStruct((n_rows, d), h.dtype),
        scratch_shapes=[pltpu.SemaphoreType.DMA(())],
        input_output_aliases={2: 0},
        name="moe_dispatch",
        compiler_params=_cparams("arbitrary"),
    )(pos, h, jnp.zeros((n_rows, d), h.dtype))


def _expert_kernel(te_ref, ts_ref, nu_ref, x_ref, wg_ref, wu_ref, wd_ref, y_ref, wg_s, wu_s, wd_s):
    j = pl.program_id(0)
    valid = j < nu_ref[0]
    new_expert = (j == 0) | (te_ref[j] != te_ref[jnp.maximum(j - 1, 0)])

    @pl.when(valid & new_expert)
    def _():
        wg_s[...] = wg_ref[0].astype(BF16)
        wu_s[...] = wu_ref[0].astype(BF16)
        wd_s[...] = wd_ref[0].astype(BF16)

    @pl.when(valid)
    def _():
        x = x_ref[...].astype(BF16)
        he = (_silu(_dot(x, wg_s[...])) * _dot(x, wu_s[...])).astype(BF16)
        y_ref[...] = _dot(he, wd_s[...])

    @pl.when(jnp.logical_not(valid))
    def _():
        y_ref[...] = jnp.zeros_like(y_ref)


def _experts(xs, tile_expert, tile_src, n_used, w_gate, w_up, w_down):
    n_rows, d = xs.shape
    n_exp, _, de = w_gate.shape
    n_tiles = n_rows // MOE_ROW_TILE
    rows = pl.BlockSpec((MOE_ROW_TILE, d), lambda j, te, ts, nu: (ts[j], 0))
    grid_spec = pltpu.PrefetchScalarGridSpec(
        num_scalar_prefetch=3,
        grid=(n_tiles,),
        in_specs=[rows,
                  pl.BlockSpec((1, d, de), lambda j, te, ts, nu: (te[j], 0, 0)),
                  pl.BlockSpec((1, d, de), lambda j, te, ts, nu: (te[j], 0, 0)),
                  pl.BlockSpec((1, de, d), lambda j, te, ts, nu: (te[j], 0, 0))],
        out_specs=pl.BlockSpec((MOE_ROW_TILE, d), lambda j, te, ts, nu: (j, 0)),
        scratch_shapes=[pltpu.VMEM((d, de), BF16), pltpu.VMEM((d, de), BF16), pltpu.VMEM((de, d), BF16)],
    )
    return pl.pallas_call(
        _expert_kernel,
        grid_spec=grid_spec,
        out_shape=jax.ShapeDtypeStruct((n_rows, d), F32),
        name="moe_experts",
        compiler_params=_cparams("arbitrary"),
    )(tile_expert, tile_src, n_used, xs, w_gate, w_up, w_down)


def _combine_kernel(pos_ref, posn_ref, route_ref, x1_ref, gpost_ref, gt2_ref, ys_hbm, o_ref, buf, sems):
    i = pl.program_id(0)
    n = pl.num_programs(0)
    tc = pos_ref.shape[2]

    def start_tile(p_ref, slot):
        def issue(r, carry):
            for k in range(2):
                pltpu.make_async_copy(ys_hbm.at[pl.ds(p_ref[0, k, r], 1)], buf.at[slot, k, pl.ds(r, 1)],
                                      sems.at[slot]).start()
            return carry
        lax.fori_loop(0, tc, issue, 0, unroll=8)

    @pl.when(i == 0)
    def _():
        start_tile(pos_ref, 0)

    @pl.when(i + 1 < n)
    def _():
        start_tile(posn_ref, (i + 1) % 2)

    slot = i % 2
    pltpu.make_async_copy(buf.at[slot], buf.at[slot], sems.at[slot]).wait()
    route = route_ref[...]
    lane = lax.broadcasted_iota(jnp.int32, route.shape, 1)
    w1 = jnp.sum(jnp.where(lane == 2, route, 0.0), axis=-1, keepdims=True)
    w2 = jnp.sum(jnp.where(lane == 3, route, 0.0), axis=-1, keepdims=True)
    y = w1 * buf[slot, 0] + w2 * buf[slot, 1]
    o_ref[...] = x1_ref[...] + gt2_ref[0] * (_rms(y) * gpost_ref[...])


def _combine(ys, pos, route, x1, gpost, gt2, seq, tc=256):
    t, d = x1.shape
    n = t // tc
    per_seq = seq // tc
    smem = lambda f: pl.BlockSpec((1, 2, tc), f, memory_space=pltpu.SMEM)
    return pl.pallas_call(
        _combine_kernel,
        grid=(n,),
        in_specs=[smem(lambda i: (i, 0, 0)), smem(lambda i: (jnp.minimum(i + 1, n - 1), 0, 0)),
                  pl.BlockSpec((tc, LANES), lambda i: (i, 0)),
                  pl.BlockSpec((tc, d), lambda i: (i, 0)),
                  pl.BlockSpec((1, d), lambda i: (0, 0)),
                  pl.BlockSpec((1, 1, d), lambda i: (i // per_seq, 0, 0)),
                  pl.BlockSpec(memory_space=pl.ANY)],
        out_specs=pl.BlockSpec((tc, d), lambda i: (i, 0)),
        out_shape=jax.ShapeDtypeStruct((t, d), F32),
        scratch_shapes=[pltpu.VMEM((2, 2, tc, d), F32), pltpu.SemaphoreType.DMA((2,))],
        name="moe_combine",
        compiler_params=_cparams("arbitrary"),
    )(pos, pos, route, x1, gpost, gt2, ys)


def _moe(h, route, counts, w_gate, w_up, w_down, x1, gpost, gt2):
    bsz, s, d = x1.shape
    t = bsz * s
    n_exp = w_gate.shape[0]
    tr = MOE_ROW_TILE
    n_tiles = 2 * t // tr + n_exp
    route = route.reshape(t, LANES)
    ids = route[:, 0:2].astype(jnp.int32)
    ranks = route[:, 4:6].astype(jnp.int32)
    cnt = counts[0, :n_exp].astype(jnp.int32)
    padded = (cnt + tr - 1) // tr * tr
    ends = jnp.cumsum(padded)
    offs = ends - padded
    onehot = ids[:, :, None] == jnp.arange(n_exp, dtype=jnp.int32)[None, None, :]
    pos = jnp.sum(jnp.where(onehot, offs[None, None, :], 0), axis=-1) + ranks
    n_used = ends[-1] // tr
    tile = jnp.arange(n_tiles, dtype=jnp.int32)
    tile_expert = jnp.sum((tile[:, None] >= (ends // tr)[None, :]).astype(jnp.int32), axis=-1)
    last = n_used - 1
    valid = tile < n_used
    tile_src = jnp.maximum(jnp.where(valid, tile, last), 0)
    tile_expert = jnp.minimum(jnp.where(valid, tile_expert, jnp.sum((last >= ends // tr).astype(jnp.int32))),
                              n_exp - 1)

    def tiles_of(tok_tile):
        return pos.reshape(t // tok_tile, tok_tile, 2).transpose(0, 2, 1)

    xs = _dispatch(h.reshape(t, d), tiles_of(512), n_tiles * tr)
    ys = _experts(xs, tile_expert, tile_src, n_used.reshape(1), w_gate, w_up, w_down)
    out = _combine(ys, tiles_of(256), route, x1.reshape(t, d), gpost, gt2, s)
    return out.reshape(bsz, s, d)


def _pad_lanes(a, n=LANES):
    return jnp.pad(a, [(0, 0)] * (a.ndim - 1) + [(0, n - a.shape[-1])])


def kernel(x, c, w_ada, b_ada, g_pre_mix, g_post_mix, w_in, w_conv_dn, a_log_dn, dt_bias_dn, g_onorm_dn, rpb_na,
           w_out, g_pre_ffn, g_post_ffn, w_group, b_group, w_expert, b_expert, w_gate, w_up, w_down):
    bsz, s, d = x.shape
    depth = w_ada.shape[0]
    n_dn = a_log_dn.shape[2]
    dn_w = n_dn * HEAD_DIM
    for l in range(depth):
        ada = _ada(c, w_ada[l], b_ada[l])
        sh1, sc1, gt1, sh2, sc2, gt2 = [a.reshape(bsz, 1, d) for a in jnp.split(ada, 6, axis=-1)]

        wi = w_in[l]
        o_z, o_ab, o_na = 3 * dn_w, 4 * dn_w, 4 * dn_w + 4 * n_dn
        wdn = wi[:, :o_z].astype(BF16)
        wz = wi[:, o_z:o_ab].astype(BF16)
        wab = _pad_lanes(wi[:, o_ab:o_na]).astype(BF16)
        na_w = (wi.shape[1] - o_na) // 3
        wna = jnp.concatenate([wi[:, o_na:o_na + na_w] * (HEAD_DIM ** -0.5), wi[:, o_na + na_w:]], axis=1).astype(BF16)
        alog = _pad_lanes(a_log_dn[l].reshape(1, 2 * n_dn))
        dtb = _pad_lanes(dt_bias_dn[l].reshape(1, 2 * n_dn))
        dn, z, gates, na = _inproj(x, sc1, sh1, g_pre_mix[l].reshape(1, d), wdn, wz, wab, wna, alog, dtb)

        gt = gates[:, :, :32].reshape(bsz, s // DN_CHUNK, DN_CHUNK, 32).transpose(0, 1, 3, 2)
        gon2 = jnp.tile(g_onorm_dn[l].reshape(1, HEAD_DIM), (1, LANES // HEAD_DIM))
        o_dn = _delta(dn, w_conv_dn[l], gates, gt, z, gon2)

        o_na = _natten(na, _na_bias_tables(rpb_na[l]))

        wo = w_out[l].astype(BF16)
        wr = _pad_lanes(jnp.concatenate([w_group[l], w_expert[l]], axis=1))
        wr_hi = wr.astype(BF16)
        wr = jnp.stack([wr_hi, (wr - wr_hi.astype(F32)).astype(BF16)])
        br = _pad_lanes(jnp.concatenate([b_group[l], b_expert[l]]).reshape(1, -1))
        x1, h2, route, counts = _outproj(o_dn, o_na, x, wo[:dn_w], wo[dn_w:], g_post_mix[l].reshape(1, d), gt1,
                                g_pre_ffn[l].reshape(1, d), sc2, sh2, wr, br)
        x = _moe(h2, route, counts, w_gate[l], w_up[l], w_down[l], x1, g_post_ffn[l].reshape(1, d), gt2)
    return x
```

```python
import numpy as np
import jax
import jax.numpy as jnp
from jax import lax
from jax.experimental import pallas as pl
from jax.experimental.pallas import tpu as pltpu

F32 = jnp.float32
BF16 = jnp.bfloat16
HIGHEST = lax.Precision.HIGHEST

GRID_W = 64
HEAD_DIM = 64
DN_CONV = 5
DN_CHUNK = 128
NA_WIN_H = 8
NA_WIN_W = 16
N_GROUPS = 4
EXPERTS_PER_GROUP = 8
N_EXPERTS = N_GROUPS * EXPERTS_PER_GROUP
EPS = 1e-6
LANES = 128
NEG_BIG = -1e30
VMEM_LIMIT = 56 * 1024 * 1024
MOE_ROW_TILE = 256


def _cparams(*sem):
    return pltpu.CompilerParams(dimension_semantics=sem, vmem_limit_bytes=VMEM_LIMIT)


def _sigmoid(x):
    return 1.0 / (1.0 + jnp.exp(-x))


def _silu(x):
    return x * _sigmoid(x)


def _softplus(x):
    return jnp.maximum(x, 0.0) + jnp.log1p(jnp.exp(-jnp.abs(x)))


def _dot(a, b, **kw):
    return jnp.dot(a, b, preferred_element_type=F32, **kw)


def _dot_nt(a, b, **kw):
    return lax.dot_general(a, b, (((1,), (1,)), ((), ())), preferred_element_type=F32, **kw)


def _dot_tn(a, b, **kw):
    return lax.dot_general(a, b, (((0,), (0,)), ((), ())), preferred_element_type=F32, **kw)


def _rms(x):
    return x * lax.rsqrt(jnp.mean(x * x, axis=-1, keepdims=True) + EPS)


def _iota2(shape):
    return lax.broadcasted_iota(jnp.int32, shape, 0), lax.broadcasted_iota(jnp.int32, shape, 1)


def _ada_kernel(c_ref, w_ref, b_ref, o_ref):
    cond = _silu(c_ref[...])
    o_ref[...] = _dot(cond, w_ref[...], precision=HIGHEST) + b_ref[...]


def _ada(c, w_ada, b_ada):
    bsz, d = c.shape
    n = w_ada.shape[1]
    tn = 1536
    return pl.pallas_call(
        _ada_kernel,
        grid=(n // tn,),
        in_specs=[pl.BlockSpec((bsz, d), lambda j: (0, 0)),
                  pl.BlockSpec((d, tn), lambda j: (0, j)),
                  pl.BlockSpec((1, tn), lambda j: (0, j))],
        out_specs=pl.BlockSpec((bsz, tn), lambda j: (0, j)),
        out_shape=jax.ShapeDtypeStruct((bsz, n), F32),
        name="ada_ln",
        compiler_params=_cparams("arbitrary"),
    )(c, w_ada, b_ada.reshape(1, n))


def _inproj_kernel(x_ref, sc_ref, sh_ref, g_ref, wdn_ref, wz_ref, wab_ref, wna_ref, alog_ref, dtb_ref,
                   dn_ref, z_ref, gate_ref, na_ref):
    h = _rms(x_ref[0]) * g_ref[...]
    h = h * (1.0 + sc_ref[0]) + sh_ref[0]
    hb = h.astype(BF16)
    dn_ref[0] = _dot(hb, wdn_ref[...]).astype(BF16)
    z_ref[0] = _dot(hb, wz_ref[...]).astype(BF16)
    na_ref[0] = _dot(hb, wna_ref[...]).astype(BF16)
    ab = _dot(hb, wab_ref[...])
    lane = lax.broadcasted_iota(jnp.int32, ab.shape, 1)
    decay = -jnp.exp(alog_ref[...]) * _softplus(ab + dtb_ref[...])
    gate_ref[0] = jnp.where(lane < 8, _chunk_cumsum(decay, reverse=False),
                            jnp.where(lane < 16, _chunk_cumsum(decay, reverse=True),
                                      _sigmoid(ab)))


def _inproj(x, sc, sh, g, wdn, wz, wab, wna, alog, dtb, tm=512):
    bsz, s, d = x.shape
    ndn, nz, nna = wdn.shape[1], wz.shape[1], wna.shape[1]
    full = lambda a: pl.BlockSpec(a.shape, lambda b, i: (0,) * a.ndim)
    mod = pl.BlockSpec((1, 1, d), lambda b, i: (b, 0, 0))
    tok = lambda n: pl.BlockSpec((1, tm, n), lambda b, i: (b, i, 0))
    return pl.pallas_call(
        _inproj_kernel,
        grid=(bsz, s // tm),
        in_specs=[tok(d), mod, mod, full(g), full(wdn), full(wz), full(wab), full(wna), full(alog), full(dtb)],
        out_specs=[tok(ndn), tok(nz), tok(LANES), tok(nna)],
        out_shape=[jax.ShapeDtypeStruct((bsz, s, ndn), BF16), jax.ShapeDtypeStruct((bsz, s, nz), BF16),
                   jax.ShapeDtypeStruct((bsz, s, LANES), F32), jax.ShapeDtypeStruct((bsz, s, nna), BF16)],
        name="prenorm_inproj",
        compiler_params=_cparams("arbitrary", "arbitrary"),
    )(x, sc, sh, g, wdn, wz, wab, wna, alog, dtb)


def _conv_silu(x, w):
    s = x.shape[0]
    row = lax.broadcasted_iota(jnp.int32, x.shape, 0)
    pad = (DN_CONV - 1) // 2
    acc = x * w[pad:pad + 1]
    for j in range(DN_CONV):
        d = j - pad
        if d == 0:
            continue
        xs = pltpu.roll(x, (-d) % s, 0)
        ok = (row + d >= 0) & (row + d < s)
        acc = acc + jnp.where(ok, xs, 0.0) * w[j:j + 1]
    return _silu(acc)


def _chunk_cumsum(g, reverse):
    s = g.shape[0]
    pos = lax.broadcasted_iota(jnp.int32, g.shape, 0) & (DN_CHUNK - 1)
    sh = 1
    while sh < DN_CHUNK:
        if reverse:
            g = g + jnp.where(pos + sh < DN_CHUNK, pltpu.roll(g, s - sh, 0), 0.0)
        else:
            g = g + jnp.where(pos >= sh, pltpu.roll(g, sh, 0), 0.0)
        sh *= 2
    return g


def _delta_kernel(q_ref, k_ref, v_ref, wq_ref, wk_ref, wv_ref, gate_ref, gt_ref, z_ref, gon_ref, o_ref,
                  k_s, q_s, kb_s, kbg_s, qd_s, kt_s, vb_s, gci_s, egl_s, m_s, t0_s, t1_s, at_s, oacc_s, w_s, u_s):
    hp = pl.program_id(1)
    s = q_ref.shape[1]
    c = DN_CHUNK
    hd = HEAD_DIM
    n_chunks = s // c

    li, lj = _iota2((LANES, LANES))
    same_head_b = lax.shift_right_logical(li, 6) == lax.shift_right_logical(lj, 6)
    same_head = same_head_b.astype(F32)

    def l2n(x):
        return x * lax.rsqrt(_dot(x * x, same_head) + EPS)

    q2 = l2n(_conv_silu(q_ref[0].astype(F32), wq_ref[...])) * (hd ** -0.5)
    k2 = l2n(_conv_silu(k_ref[0].astype(F32), wk_ref[...]))
    v2 = _conv_silu(v_ref[0].astype(F32), wv_ref[...])
    k_s[...] = k2.astype(BF16)
    q_s[...] = q2.astype(BF16)

    gates = gate_ref[0]
    lane_s = lax.broadcasted_iota(jnp.int32, gates.shape, 1)

    def pair_cols(base):
        c0 = base + 2 * hp
        col0 = jnp.sum(jnp.where(lane_s == c0, gates, 0.0), axis=-1, keepdims=True)
        col1 = jnp.sum(jnp.where(lane_s == c0 + 1, gates, 0.0), axis=-1, keepdims=True)
        return jnp.where(lane_s < hd, col0, col1)

    ri, rj = _iota2((c, c))
    eye = (ri == rj).astype(F32)

    for direction in range(2):
        gci2 = pair_cols(8 * direction)
        beta2 = pair_cols(16 + 8 * direction)
        gc3 = gci2.reshape(n_chunks, c, LANES)
        tot = gc3[:, c - 1:c, :] if direction == 0 else gc3[:, 0:1, :]
        gl2 = jnp.broadcast_to(tot, gc3.shape).reshape(s, LANES)
        egc2 = jnp.exp(gci2)
        kb2 = k2 * beta2
        kb_s[direction] = kb2.astype(BF16)
        kbg_s[direction] = (kb2 * egc2).astype(BF16)
        qd_s[direction] = (q2 * egc2).astype(BF16)
        kt_s[direction] = (k2 * jnp.exp(gl2 - gci2)).astype(BF16)
        vb_s[direction] = v2 * beta2
        gci_s[direction] = gci2
        egl_s[direction] = jnp.exp(gl2)

    lane_c = lax.broadcasted_iota(jnp.int32, (c, LANES), 1)
    head0 = lane_c < hd

    same = lambda sh: lax.shift_right_logical(ri, sh) == lax.shift_right_logical(rj, sh)
    group = 16
    n_levels = c.bit_length() - 2
    t_bufs = (t0_s, t1_s)

    def chunk_rows(g, j):
        return pl.ds(pl.multiple_of((g * group + j) * c, c), c)

    def chunk_matrices(g, carry):
        for j in range(group):
            rows = chunk_rows(g, j)
            kc = k_s[rows, :]
            qc = q_s[rows, :]
            zero = jnp.zeros_like(kc)
            for direction in range(2):
                incl = (ri >= rj) if direction == 0 else (ri <= rj)
                strict = (ri > rj) if direction == 0 else (ri < rj)
                kbc = kb_s[direction, rows, :]
                gci = gci_s[direction, rows, :]
                for hh in range(2):
                    hm = head0 if hh == 0 else jnp.logical_not(head0)
                    gi = jnp.broadcast_to(gci[:, hh * hd:hh * hd + 1], (c, c))
                    gj = jnp.broadcast_to(
                        gt_ref[0, g * group + j, pl.ds(8 * direction + 2 * hp + hh, 1), :], (c, c))
                    decay = jnp.where(incl, jnp.exp(jnp.where(incl, gi - gj, 0.0)), 0.0)
                    m = jnp.where(strict, _dot_nt(jnp.where(hm, kbc, zero), kc) * decay, 0.0)
                    m_s[direction, hh, rows, :] = m.astype(BF16)
                    t_bufs[n_levels % 2][direction, hh, rows, :] =(eye - jnp.where(same(1), m, 0.0)).astype(BF16)
                    at_s[direction, hh, rows, :] = (_dot_nt(jnp.where(hm, qc, zero), kc) * decay).astype(BF16)
        return carry

    lax.fori_loop(0, n_chunks // group, chunk_matrices, 0)

    for level in range(n_levels):
        def merge_level(g, carry, level=level):
            sh = level + 1
            src = (n_levels - level) % 2
            cmask = same(sh + 1) & jnp.logical_not(same(sh))
            for j in range(group):
                rows = chunk_rows(g, j)
                for direction in range(2):
                    for hh in range(2):
                        tb = t_bufs[src][direction, hh, rows, :]
                        mb = m_s[direction, hh, rows, :]
                        cm = jnp.where(cmask, mb, jnp.zeros_like(mb))
                        y = _dot(tb, _dot(cm, tb).astype(BF16))
                        t_bufs[1 - src][direction, hh, rows, :] =jnp.where(cmask, (-y).astype(BF16), tb)
            return carry

        lax.fori_loop(0, n_chunks // group, merge_level, 0)

    def apply_inverse(g, carry):
        for j in range(group):
            rows = chunk_rows(g, j)
            for direction in range(2):
                kbg = kbg_s[direction, rows, :]
                vb = vb_s[direction, rows, :].astype(BF16)
                th0 = t0_s[direction, 0, rows, :]
                th1 = t0_s[direction, 1, rows, :]
                w_s[direction, rows, :] = jnp.where(head0, _dot(th0, kbg), _dot(th1, kbg)).astype(BF16)
                u_s[direction, rows, :] = jnp.where(head0, _dot(th0, vb), _dot(th1, vb))
        return carry

    lax.fori_loop(0, n_chunks // group, apply_inverse, 0)

    oacc_s[...] = jnp.zeros_like(oacc_s)

    def scan_step(i, states):
        new_states = []
        for direction in range(2):
            n = i if direction == 0 else n_chunks - 1 - i
            r0 = pl.multiple_of(n * c, c)
            rows = pl.ds(r0, c)
            sb = states[direction].astype(BF16)
            v_new = (u_s[direction, rows, :] - _dot(w_s[direction, rows, :], sb)).astype(BF16)
            intra = jnp.where(head0, _dot(at_s[direction, 0, rows, :], v_new),
                              _dot(at_s[direction, 1, rows, :], v_new))
            oacc_s[rows, :] += _dot(qd_s[direction, rows, :], sb) + intra
            upd = _dot_tn(kt_s[direction, rows, :], v_new)
            new_states.append(states[direction] * egl_s[direction, pl.ds(r0, 1), :]
                              + jnp.where(same_head_b, upd, 0.0))
        return tuple(new_states)

    zero_state = jnp.zeros((LANES, LANES), F32)
    lax.fori_loop(0, n_chunks, scan_step, (zero_state, zero_state), unroll=True)

    o2 = oacc_s[...]
    ms = _dot(o2 * o2, same_head) * (1.0 / hd)
    o2 = o2 * lax.rsqrt(ms + EPS) * gon_ref[...]
    o_ref[0] = (o2 * _silu(z_ref[0].astype(F32))).astype(BF16)


def _delta(dn, w_conv, gates, gt, z, gon2):
    bsz, s, _ = dn.shape
    n_pairs = z.shape[2] // LANES
    c = DN_CHUNK
    col = lambda off: pl.BlockSpec((1, s, LANES), lambda b, p: (b, 0, off + p))
    wcol = lambda off: pl.BlockSpec((DN_CONV, LANES), lambda b, p: (0, off + p))
    two = lambda dt: pltpu.VMEM((2, s, LANES), dt)
    return pl.pallas_call(
        _delta_kernel,
        grid=(bsz, n_pairs),
        in_specs=[col(0), col(n_pairs), col(2 * n_pairs), wcol(0), wcol(n_pairs), wcol(2 * n_pairs),
                  pl.BlockSpec((1, s, LANES), lambda b, p: (b, 0, 0)),
                  pl.BlockSpec((1, s // c, 32, c), lambda b, p: (b, 0, 0, 0)),
                  col(0),
                  pl.BlockSpec((1, LANES), lambda b, p: (0, 0))],
        out_specs=col(0),
        out_shape=jax.ShapeDtypeStruct((bsz, s, n_pairs * LANES), BF16),
        scratch_shapes=[pltpu.VMEM((s, LANES), BF16), pltpu.VMEM((s, LANES), BF16),
                        two(BF16), two(BF16), two(BF16), two(BF16),
                        two(F32), two(F32), two(F32),
                        pltpu.VMEM((2, 2, s, LANES), BF16), pltpu.VMEM((2, 2, s, LANES), BF16),
                        pltpu.VMEM((2, 2, s, LANES), BF16), pltpu.VMEM((2, 2, s, LANES), BF16),
                        pltpu.VMEM((s, LANES), F32), two(BF16), two(F32)],
        name="delta_rule",
        compiler_params=_cparams("arbitrary", "arbitrary"),
    )(dn, dn, dn, w_conv, w_conv, w_conv, gates, gt, z, gon2)


def _na_bias_tables(rpb):
    n_heads = rpb.shape[0]
    cc = np.arange(GRID_W)
    c0 = np.clip(cc - NA_WIN_W // 2, 0, GRID_W - NA_WIN_W)
    kc = np.arange(GRID_W)
    in_win = (kc[None, :] >= c0[:, None]) & (kc[None, :] < c0[:, None] + NA_WIN_W)
    col_off = kc[None, :] - cc[:, None] + NA_WIN_W - 1
    onehot = (col_off[None] == np.arange(2 * NA_WIN_W - 1)[:, None, None]) & in_win[None]
    rows = jnp.stack([rpb[:, NA_WIN_H - 1 - d:2 * NA_WIN_H - 1 - d, :] for d in range(NA_WIN_H)], axis=1)
    rows = rows.reshape(n_heads // 2, 2, NA_WIN_H, NA_WIN_H, 2 * NA_WIN_W - 1).astype(F32)
    bias = jnp.einsum("phdic,cqk->pdhqik", rows, jnp.asarray(onehot, F32), precision=HIGHEST)
    bias = bias + jnp.asarray(np.where(in_win, 0.0, NEG_BIG), F32)[None, None, None, :, None, :]
    return bias.reshape(n_heads // 2, NA_WIN_H, 2 * GRID_W, NA_WIN_H * GRID_W)


def _natten_kernel(q_ref, k_ref, v_ref, bias_ref, o_ref, sc_s, p_s):
    s = q_ref.shape[1]
    rows = s // GRID_W
    w = GRID_W
    win = min(NA_WIN_H, rows)
    hd = HEAD_DIM
    lane = lax.broadcasted_iota(jnp.int32, (w, LANES), 1)
    first = lane < hd
    window = lambda r: min(max(r - win // 2, 0), rows - win)
    for r in range(rows):
        r0 = window(r)
        q = q_ref[0, r * w:(r + 1) * w, :]
        zero = jnp.zeros_like(q)
        q_st = jnp.concatenate([jnp.where(first, q, zero), jnp.where(first, zero, q)], axis=0)
        kw = k_ref[0, r0 * w:(r0 + win) * w, :]
        sc_s[r] = _dot_nt(q_st, kw) + bias_ref[0, r - r0]
    for r in range(rows):
        sc = sc_s[r]
        p = jnp.exp(sc - jnp.max(sc, axis=-1, keepdims=True))
        p_s[r] = (p * (1.0 / jnp.sum(p, axis=-1, keepdims=True))).astype(BF16)
    for r in range(rows):
        r0 = window(r)
        o = _dot(p_s[r], v_ref[0, r0 * w:(r0 + win) * w, :])
        o_ref[0, r * w:(r + 1) * w, :] = jnp.where(first, o[:w], o[w:]).astype(BF16)


def _natten(na, bias):
    bsz, s, n3 = na.shape
    n_pairs = n3 // (3 * LANES)
    col = lambda off: pl.BlockSpec((1, s, LANES), lambda p, b: (b, 0, off + p))
    return pl.pallas_call(
        _natten_kernel,
        grid=(n_pairs, bsz),
        in_specs=[col(0), col(n_pairs), col(2 * n_pairs),
                  pl.BlockSpec((1,) + bias.shape[1:], lambda p, b: (p, 0, 0, 0))],
        out_specs=col(0),
        out_shape=jax.ShapeDtypeStruct((bsz, s, n_pairs * LANES), BF16),
        scratch_shapes=[pltpu.VMEM((s // GRID_W,) + bias.shape[2:], F32),
                        pltpu.VMEM((s // GRID_W,) + bias.shape[2:], BF16)],
        name="natten",
        compiler_params=_cparams("arbitrary", "arbitrary"),
    )(na, na, na, bias)


def _outproj_kernel(odn_ref, ona_ref, x_ref, wo_dn_ref, wo_na_ref, gpost_ref, gt1_ref, gpre_ref, sc_ref, sh_ref,
                    wr_ref, br_ref, x1_ref, h_ref, route_ref, cnt_ref, cnt_s):
    y = _dot(odn_ref[0], wo_dn_ref[...]) + _dot(ona_ref[0], wo_na_ref[...])
    x1 = x_ref[0] + gt1_ref[0] * (_rms(y) * gpost_ref[...])
    x1_ref[0] = x1
    h = _rms(x1) * gpre_ref[...]
    h = h * (1.0 + sc_ref[0]) + sh_ref[0]
    h_ref[0] = h

    h_hi = h.astype(BF16)
    h_lo = (h - h_hi.astype(F32)).astype(BF16)
    logits = (_dot(h_hi, wr_ref[0]) + (_dot(h_lo, wr_ref[0]) + _dot(h_hi, wr_ref[1]))
              + br_ref[...])
    lane = lax.broadcasted_iota(jnp.int32, logits.shape, 1).astype(F32)
    big = float(LANES)

    def masked_softmax(mask):
        lm = jnp.where(mask, logits, NEG_BIG)
        e = jnp.where(mask, jnp.exp(lm - jnp.max(lm, axis=-1, keepdims=True)), 0.0)
        return e / jnp.sum(e, axis=-1, keepdims=True)

    def top1(p, mask):
        pm = jnp.where(mask, p, -1.0)
        best = jnp.max(pm, axis=-1, keepdims=True)
        idx = jnp.min(jnp.where(mask & (pm == best), lane, big), axis=-1, keepdims=True)
        return best, idx

    gmask = lane < N_GROUPS
    gp_top, g_idx = top1(masked_softmax(gmask), gmask)
    e_lo = N_GROUPS + g_idx * EXPERTS_PER_GROUP
    emask = (lane >= e_lo) & (lane < e_lo + EXPERTS_PER_GROUP)
    pe = masked_softmax(emask)
    p1, i1 = top1(pe, emask)
    p2, i2 = top1(pe, emask & (lane != i1))
    w1 = gp_top * p1 / (p1 + p2)
    w2 = gp_top * p2 / (p1 + p2)

    @pl.when((pl.program_id(0) == 0) & (pl.program_id(1) == 0))
    def _():
        cnt_s[...] = jnp.zeros_like(cnt_s)

    el = lane + N_GROUPS
    oh1 = el == i1
    oh2 = el == i2
    tm = logits.shape[0]
    ti, tj = _iota2((tm, tm))
    before = (ti > tj).astype(BF16)
    pre1 = _dot(before, oh1.astype(BF16))
    pre2 = _dot(before, oh2.astype(BF16))
    tot1 = jnp.sum(oh1.astype(F32), axis=0, keepdims=True)
    tot2 = jnp.sum(oh2.astype(F32), axis=0, keepdims=True)
    base = cnt_s[...]
    rank1 = jnp.sum(jnp.where(oh1, base + pre1, 0.0), axis=-1, keepdims=True)
    rank2 = jnp.sum(jnp.where(oh2, base + tot1 + pre2, 0.0), axis=-1, keepdims=True)
    cnt_s[...] = base + tot1 + tot2
    cnt_ref[...] = cnt_s[...]
    fields = (i1 - N_GROUPS, i2 - N_GROUPS, w1, w2, rank1, rank2)
    route = jnp.zeros_like(logits)
    for n, f in enumerate(fields):
        route = jnp.where(lane == n, f, route)
    route_ref[0] = route


def _outproj(odn, ona, x, wo_dn, wo_na, gpost, gt1, gpre, sc2, sh2, wr, br, tm=512):
    bsz, s, d = x.shape
    full = lambda a: pl.BlockSpec(a.shape, lambda b, i: (0,) * a.ndim)
    mod = pl.BlockSpec((1, 1, d), lambda b, i: (b, 0, 0))
    tok = lambda n: pl.BlockSpec((1, tm, n), lambda b, i: (b, i, 0))
    return pl.pallas_call(
        _outproj_kernel,
        grid=(bsz, s // tm),
        in_specs=[tok(odn.shape[2]), tok(ona.shape[2]), tok(d), full(wo_dn), full(wo_na), full(gpost), mod,
                  full(gpre), mod, mod, full(wr), full(br)],
        out_specs=[tok(d), tok(d), tok(LANES), pl.BlockSpec((1, LANES), lambda b, i: (0, 0))],
        out_shape=[jax.ShapeDtypeStruct((bsz, s, d), F32), jax.ShapeDtypeStruct((bsz, s, d), F32),
                   jax.ShapeDtypeStruct((bsz, s, LANES), F32), jax.ShapeDtypeStruct((1, LANES), F32)],
        scratch_shapes=[pltpu.VMEM((1, LANES), F32)],
        name="outproj_route",
        compiler_params=_cparams("arbitrary", "arbitrary"),
    )(odn, ona, x, wo_dn, wo_na, gpost, gt1, gpre, sc2, sh2, wr, br)


def _dispatch_kernel(tail_ref, pos_ref, h_ref, xs_hbm, zero_s, sem):
    tb = pos_ref.shape[2]
    tr = zero_s.shape[0]

    @pl.when(pl.program_id(0) == 0)
    def _():
        n_exp = tail_ref.shape[0] - 1
        n_rows = xs_hbm.shape[0]
        zero_s[...] = jnp.zeros_like(zero_s)
        targets = [(tail_ref[e] >= 0, tail_ref[e]) for e in range(n_exp)]
        targets += [(n_rows - (j + 1) * tr >= tail_ref[n_exp], n_rows - (j + 1) * tr) for j in range(n_exp)]
        for cond, row in targets:
            @pl.when(cond)
            def _(row=row):
                start = row if isinstance(row, int) else pl.multiple_of(row, 8)
                pltpu.make_async_copy(zero_s, xs_hbm.at[pl.ds(start, tr)], sem).start()
        for cond, _ in targets:
            @pl.when(cond)
            def _():
                pltpu.make_async_copy(zero_s, xs_hbm.at[pl.ds(0, tr)], sem).wait()

    def issue(r, carry):
        src = h_ref.at[pl.ds(r, 1)]
        for slot in range(2):
            pltpu.make_async_copy(src, xs_hbm.at[pl.ds(pos_ref[0, slot, r], 1)], sem).start()
        return carry

    lax.fori_loop(0, tb, issue, 0, unroll=8)
    for _ in range(2):
        pltpu.make_async_copy(h_ref, xs_hbm.at[pl.ds(0, tb)], sem).wait()


def _dispatch(h, pos, tails, n_rows, tb=512):
    t, d = h.shape
    grid_spec = pltpu.PrefetchScalarGridSpec(
        num_scalar_prefetch=1,
        grid=(t // tb,),
        in_specs=[pl.BlockSpec((1, 2, tb), lambda i, tl: (i, 0, 0), memory_space=pltpu.SMEM),
                  pl.BlockSpec((tb, d), lambda i, tl: (i, 0))],
        out_specs=pl.BlockSpec(memory_space=pl.ANY),
        scratch_shapes=[pltpu.VMEM((MOE_ROW_TILE, d), h.dtype), pltpu.SemaphoreType.DMA(())],
    )
    return pl.pallas_call(
        _dispatch_kernel,
        grid_spec=grid_spec,
        out_shape=jax.ShapeDtypeStruct((n_rows, d), h.dtype),
        name="moe_dispatch",
        compiler_params=_cparams("arbitrary"),
    )(tails, pos, h)


def _expert_kernel(te_ref, ts_ref, nu_ref, x_ref, wg_ref, wu_ref, wd_ref, y_ref, wg_s, wu_s, wd_s):
    j = pl.program_id(0)
    valid = j < nu_ref[0]
    new_expert = (j == 0) | (te_ref[j] != te_ref[jnp.maximum(j - 1, 0)])

    @pl.when(valid & new_expert)
    def _():
        wg_s[...] = wg_ref[0].astype(BF16)
        wu_s[...] = wu_ref[0].astype(BF16)
        wd_s[...] = wd_ref[0].astype(BF16)

    @pl.when(valid)
    def _():
        x = x_ref[...].astype(BF16)
        he = (_silu(_dot(x, wg_s[...])) * _dot(x, wu_s[...])).astype(BF16)
        y_ref[...] = _dot(he, wd_s[...])

    @pl.when(jnp.logical_not(valid))
    def _():
        y_ref[...] = jnp.zeros_like(y_ref)


def _experts(xs, tile_expert, tile_src, n_used, w_gate, w_up, w_down):
    n_rows, d = xs.shape
    n_exp, _, de = w_gate.shape
    n_tiles = n_rows // MOE_ROW_TILE
    rows = pl.BlockSpec((MOE_ROW_TILE, d), lambda j, te, ts, nu: (ts[j], 0))
    grid_spec = pltpu.PrefetchScalarGridSpec(
        num_scalar_prefetch=3,
        grid=(n_tiles,),
        in_specs=[rows,
                  pl.BlockSpec((1, d, de), lambda j, te, ts, nu: (te[j], 0, 0)),
                  pl.BlockSpec((1, d, de), lambda j, te, ts, nu: (te[j], 0, 0)),
                  pl.BlockSpec((1, de, d), lambda j, te, ts, nu: (te[j], 0, 0))],
        out_specs=pl.BlockSpec((MOE_ROW_TILE, d), lambda j, te, ts, nu: (j, 0)),
        scratch_shapes=[pltpu.VMEM((d, de), BF16), pltpu.VMEM((d, de), BF16), pltpu.VMEM((de, d), BF16)],
    )
    return pl.pallas_call(
        _expert_kernel,
        grid_spec=grid_spec,
        out_shape=jax.ShapeDtypeStruct((n_rows, d), F32),
        name="moe_experts",
        compiler_params=_cparams("arbitrary"),
    )(tile_expert, tile_src, n_used, xs, w_gate, w_up, w_down)


def _combine_kernel(pos_ref, posn_ref, route_ref, x1_ref, gpost_ref, gt2_ref, ys_hbm, o_ref, buf, sems):
    i = pl.program_id(0)
    n = pl.num_programs(0)
    tc = pos_ref.shape[2]

    def start_tile(p_ref, slot):
        def issue(r, carry):
            for k in range(2):
                pltpu.make_async_copy(ys_hbm.at[pl.ds(p_ref[0, k, r], 1)], buf.at[slot, k, pl.ds(r, 1)],
                                      sems.at[slot]).start()
            return carry
        lax.fori_loop(0, tc, issue, 0, unroll=8)

    @pl.when(i == 0)
    def _():
        start_tile(pos_ref, 0)

    @pl.when(i + 1 < n)
    def _():
        start_tile(posn_ref, (i + 1) % 2)

    slot = i % 2
    pltpu.make_async_copy(buf.at[slot], buf.at[slot], sems.at[slot]).wait()
    route = route_ref[...]
    lane = lax.broadcasted_iota(jnp.int32, route.shape, 1)
    w1 = jnp.sum(jnp.where(lane == 2, route, 0.0), axis=-1, keepdims=True)
    w2 = jnp.sum(jnp.where(lane == 3, route, 0.0), axis=-1, keepdims=True)
    y = w1 * buf[slot, 0] + w2 * buf[slot, 1]
    o_ref[...] = x1_ref[...] + gt2_ref[0] * (_rms(y) * gpost_ref[...])


def _combine(ys, pos, route, x1, gpost, gt2, seq, tc=256):
    t, d = x1.shape
    n = t // tc
    per_seq = seq // tc
    smem = lambda f: pl.BlockSpec((1, 2, tc), f, memory_space=pltpu.SMEM)
    return pl.pallas_call(
        _combine_kernel,
        grid=(n,),
        in_specs=[smem(lambda i: (i, 0, 0)), smem(lambda i: (jnp.minimum(i + 1, n - 1), 0, 0)),
                  pl.BlockSpec((tc, LANES), lambda i: (i, 0)),
                  pl.BlockSpec((tc, d), lambda i: (i, 0)),
                  pl.BlockSpec((1, d), lambda i: (0, 0)),
                  pl.BlockSpec((1, 1, d), lambda i: (i // per_seq, 0, 0)),
                  pl.BlockSpec(memory_space=pl.ANY)],
        out_specs=pl.BlockSpec((tc, d), lambda i: (i, 0)),
        out_shape=jax.ShapeDtypeStruct((t, d), F32),
        scratch_shapes=[pltpu.VMEM((2, 2, tc, d), F32), pltpu.SemaphoreType.DMA((2,))],
        name="moe_combine",
        compiler_params=_cparams("arbitrary"),
    )(pos, pos, route, x1, gpost, gt2, ys)


def _moe(h, route, counts, w_gate, w_up, w_down, x1, gpost, gt2):
    bsz, s, d = x1.shape
    t = bsz * s
    n_exp = w_gate.shape[0]
    tr = MOE_ROW_TILE
    n_tiles = 2 * t // tr + n_exp
    route = route.reshape(t, LANES)
    ids = route[:, 0:2].astype(jnp.int32)
    ranks = route[:, 4:6].astype(jnp.int32)
    cnt = counts[0, :n_exp].astype(jnp.int32)
    padded = (cnt + tr - 1) // tr * tr
    ends = jnp.cumsum(padded)
    offs = ends - padded
    onehot = ids[:, :, None] == jnp.arange(n_exp, dtype=jnp.int32)[None, None, :]
    pos = jnp.sum(jnp.where(onehot, offs[None, None, :], 0), axis=-1) + ranks
    n_used = ends[-1] // tr
    tile = jnp.arange(n_tiles, dtype=jnp.int32)
    tile_expert = jnp.sum((tile[:, None] >= (ends // tr)[None, :]).astype(jnp.int32), axis=-1)
    last = n_used - 1
    valid = tile < n_used
    tile_src = jnp.maximum(jnp.where(valid, tile, last), 0)
    tile_expert = jnp.minimum(jnp.where(valid, tile_expert, jnp.sum((last >= ends // tr).astype(jnp.int32))),
                              n_exp - 1)

    def tiles_of(tok_tile):
        return pos.reshape(t // tok_tile, tok_tile, 2).transpose(0, 2, 1)

    tails = jnp.concatenate([jnp.where(cnt > 0, ends - tr, -1), ends[-1:]])
    xs = _dispatch(h.reshape(t, d), tiles_of(512), tails, n_tiles * tr)
    ys = _experts(xs, tile_expert, tile_src, n_used.reshape(1), w_gate, w_up, w_down)
    out = _combine(ys, tiles_of(256), route, x1.reshape(t, d), gpost, gt2, s)
    return out.reshape(bsz, s, d)


def _pad_lanes(a, n=LANES):
    return jnp.pad(a, [(0, 0)] * (a.ndim - 1) + [(0, n - a.shape[-1])])


def kernel(x, c, w_ada, b_ada, g_pre_mix, g_post_mix, w_in, w_conv_dn, a_log_dn, dt_bias_dn, g_onorm_dn, rpb_na,
           w_out, g_pre_ffn, g_post_ffn, w_group, b_group, w_expert, b_expert, w_gate, w_up, w_down):
    bsz, s, d = x.shape
    depth = w_ada.shape[0]
    n_dn = a_log_dn.shape[2]
    dn_w = n_dn * HEAD_DIM
    for l in range(depth):
        ada = _ada(c, w_ada[l], b_ada[l])
        sh1, sc1, gt1, sh2, sc2, gt2 = [a.reshape(bsz, 1, d) for a in jnp.split(ada, 6, axis=-1)]

        wi = w_in[l]
        o_z, o_ab, o_na = 3 * dn_w, 4 * dn_w, 4 * dn_w + 4 * n_dn
        wdn = wi[:, :o_z].astype(BF16)
        wz = wi[:, o_z:o_ab].astype(BF16)
        wab = _pad_lanes(wi[:, o_ab:o_na]).astype(BF16)
        na_w = (wi.shape[1] - o_na) // 3
        wna = jnp.concatenate([wi[:, o_na:o_na + na_w] * (HEAD_DIM ** -0.5), wi[:, o_na + na_w:]], axis=1).astype(BF16)
        alog = _pad_lanes(a_log_dn[l].reshape(1, 2 * n_dn))
        dtb = _pad_lanes(dt_bias_dn[l].reshape(1, 2 * n_dn))
        dn, z, gates, na = _inproj(x, sc1, sh1, g_pre_mix[l].reshape(1, d), wdn, wz, wab, wna, alog, dtb)

        gt = gates[:, :, :32].reshape(bsz, s // DN_CHUNK, DN_CHUNK, 32).transpose(0, 1, 3, 2)
        gon2 = jnp.tile(g_onorm_dn[l].reshape(1, HEAD_DIM), (1, LANES // HEAD_DIM))
        o_dn = _delta(dn, w_conv_dn[l], gates, gt, z, gon2)

        o_na = _natten(na, _na_bias_tables(rpb_na[l]))

        wo = w_out[l].astype(BF16)
        wr = _pad_lanes(jnp.concatenate([w_group[l], w_expert[l]], axis=1))
        wr_hi = wr.astype(BF16)
        wr = jnp.stack([wr_hi, (wr - wr_hi.astype(F32)).astype(BF16)])
        br = _pad_lanes(jnp.concatenate([b_group[l], b_expert[l]]).reshape(1, -1))
        x1, h2, route, counts = _outproj(o_dn, o_na, x, wo[:dn_w], wo[dn_w:], g_post_mix[l].reshape(1, d), gt1,
                                g_pre_ffn[l].reshape(1, d), sc2, sh2, wr, br)
        x = _moe(h2, route, counts, w_gate[l], w_up[l], w_down[l], x1, g_post_ffn[l].reshape(1, d), gt2)
    return x
```
